```python
import math
import jax, jax.numpy as jnp
from jax import lax
import numpy as np

D_MODEL = 1024
BATCH = 32
SEQ = 2048
DEPTH = 1

N_META = 16
EPS = 1e-6

GLA_HEADS = 4
GLA_DK = 128
GLA_DV = 256
GLA_GATE_RANK = 16
GLA_GATE_NORMALIZER = 16.0
GLA_CHUNK = 64
GLA_KW = GLA_HEADS * GLA_DK
GLA_VW = GLA_HEADS * GLA_DV

MLA_HEADS = 8
MLA_NOPE = 128
MLA_ROPE = 64
MLA_DV = 128
MLA_Q_RANK = 256
MLA_KV_RANK = 128
MLA_QK = MLA_NOPE + MLA_ROPE
MLA_VW = MLA_HEADS * MLA_DV
ROPE_BASE = 10000.0
ATTN_BLOCK = 128

SPLITS = (GLA_KW, GLA_KW, GLA_VW, GLA_GATE_RANK, GLA_VW,
          MLA_Q_RANK, MLA_KV_RANK, MLA_ROPE, MLA_VW,
          D_MODEL, D_MODEL)
IN_WIDTH = (2 * GLA_KW + 2 * GLA_VW + GLA_GATE_RANK + MLA_Q_RANK + MLA_KV_RANK
            + MLA_ROPE + MLA_VW + 2 * D_MODEL)

kernel_name = "hybrid_gla_mla_gated_meta"


def rms_norm(x, g):
    xf = x.astype(jnp.float32)
    y = xf * lax.rsqrt(jnp.mean(xf * xf, axis=-1, keepdims=True) + EPS)
    return (y * g.astype(jnp.float32)).astype(x.dtype)


def rope_tables(n, dim):
    inv = 1.0 / (ROPE_BASE ** (jnp.arange(0, dim, 2, dtype=jnp.float32) / dim))
    ang = jnp.arange(n, dtype=jnp.float32)[:, None] * inv[None, :]
    return jnp.cos(ang), jnp.sin(ang)


def apply_rope(x, cos, sin):
    xf = x.astype(jnp.float32)
    x1, x2 = jnp.split(xf, 2, axis=-1)
    return jnp.concatenate([x1 * cos - x2 * sin, x2 * cos + x1 * sin], axis=-1).astype(x.dtype)


def gla_chunked(q, k, v, g):
    B, L, H, dk = q.shape
    dv = v.shape[-1]
    C = GLA_CHUNK
    front = (-N_META) % C
    back = (-(L - N_META)) % C
    padw = ((0, 0), (front, back), (0, 0), (0, 0))
    q, k, v, g = [jnp.pad(t.astype(jnp.float32), padw) for t in (q, k, v, g)]
    N = q.shape[1] // C

    def chunks(t):
        return t.reshape(B, N, C, H, t.shape[-1]).transpose(1, 0, 3, 2, 4)

    q, k, v, g = chunks(q), chunks(k), chunks(v), chunks(g)
    b = jnp.cumsum(g, axis=3)
    b_last = b[:, :, :, -1:, :]
    qe = q * jnp.exp(b)
    ke = k * jnp.exp(-b)
    kl = k * jnp.exp(b_last - b)
    mask = jnp.tril(jnp.ones((C, C), dtype=bool))
    A = jnp.where(mask, jnp.einsum('nbhid,nbhjd->nbhij', qe, ke), 0.0)
    o_intra = jnp.einsum('nbhij,nbhjv->nbhiv', A, v)
    decay = jnp.exp(b_last[:, :, :, 0, :])

    def step(S, inp):
        qe_n, kl_n, v_n, d_n = inp
        o = jnp.einsum('bhid,bhdv->bhiv', qe_n, S)
        S = S * d_n[..., None] + jnp.einsum('bhjd,bhjv->bhdv', kl_n, v_n)
        return S, o

    S0 = jnp.zeros((B, H, dk, dv), jnp.float32)
    _, o_inter = lax.scan(step, S0, (qe, kl, v, decay))
    o = (o_intra + o_inter).transpose(1, 0, 3, 2, 4).reshape(B, N * C, H, dv)
    return o[:, front:front + L]


def mla_attention(c_q, c_kv, k_rope, q_norm_g, w_uq, kv_norm_g, w_ukv):
    B, L, _ = c_q.shape
    H = MLA_HEADS
    cos, sin = rope_tables(L, MLA_ROPE)
    q = (rms_norm(c_q, q_norm_g) @ w_uq).reshape(B, L, H, MLA_QK)
    q_nope, q_rope = q[..., :MLA_NOPE], q[..., MLA_NOPE:]
    q_rope = apply_rope(q_rope, cos[:, None, :], sin[:, None, :])
    kv = (rms_norm(c_kv, kv_norm_g) @ w_ukv).reshape(B, L, H, MLA_NOPE + MLA_DV)
    k_nope, v = kv[..., :MLA_NOPE], kv[..., MLA_NOPE:]
    k_rope = apply_rope(k_rope, cos, sin)
    k = jnp.concatenate([k_nope, jnp.broadcast_to(k_rope[:, :, None, :], (B, L, H, MLA_ROPE))], axis=-1)
    q = jnp.concatenate([q_nope, q_rope], axis=-1)

    Lp = ((L + ATTN_BLOCK - 1) // ATTN_BLOCK) * ATTN_BLOCK
    nb = Lp // ATTN_BLOCK
    padw = ((0, 0), (0, Lp - L), (0, 0), (0, 0))
    q = jnp.pad(q, padw).transpose(0, 2, 1, 3)
    k = jnp.pad(k, padw).transpose(0, 2, 1, 3)
    v = jnp.pad(v, padw).transpose(0, 2, 1, 3)
    qb = q.reshape(B, H, nb, ATTN_BLOCK, MLA_QK).transpose(2, 0, 1, 3, 4)
    scale = 1.0 / math.sqrt(MLA_QK)
    kpos = jnp.arange(Lp)

    def block(args):
        q_blk, i = args
        s = jnp.einsum('bhqd,bhkd->bhqk', q_blk, k).astype(jnp.float32) * scale
        qpos = i * ATTN_BLOCK + jnp.arange(ATTN_BLOCK)
        s = jnp.where(kpos[None, :] <= qpos[:, None], s, -jnp.inf)
        p = jax.nn.softmax(s, axis=-1)
        return jnp.einsum('bhqk,bhkv->bhqv', p.astype(v.dtype), v)

    o = lax.map(block, (qb, jnp.arange(nb)))
    o = o.transpose(1, 0, 3, 2, 4).reshape(B, Lp, H * MLA_DV)
    return o[:, :L]


def setup_inputs(seed: int = 0) -> dict:
    key = jax.random.key(seed)
    ks = jax.random.split(key, 16)
    f = jnp.float32
    n = lambda k, s, sc: jax.random.normal(k, s, f) * sc
    return {
        "x": n(ks[0], (BATCH, SEQ, D_MODEL), 1.0),
        "meta_tokens": n(ks[1], (N_META, D_MODEL), 1.0),
        "norm_g": 1.0 + n(ks[2], (DEPTH, D_MODEL), 0.02),
        "w_in": n(ks[3], (DEPTH, D_MODEL, IN_WIDTH), D_MODEL ** -0.5),
        "gla_gate_w": n(ks[4], (DEPTH, GLA_GATE_RANK, GLA_KW), GLA_GATE_RANK ** -0.5),
        "gla_gate_b": n(ks[5], (DEPTH, GLA_KW), 0.1),
        "gla_norm_g": 1.0 + n(ks[6], (DEPTH, GLA_DV), 0.02),
        "gla_proj": n(ks[7], (DEPTH, GLA_VW, D_MODEL), GLA_VW ** -0.5),
        "mla_q_norm_g": 1.0 + n(ks[8], (DEPTH, MLA_Q_RANK), 0.02),
        "mla_w_uq": n(ks[9], (DEPTH, MLA_Q_RANK, MLA_HEADS * MLA_QK), MLA_Q_RANK ** -0.5),
        "mla_kv_norm_g": 1.0 + n(ks[10], (DEPTH, MLA_KV_RANK), 0.02),
        "mla_w_ukv": n(ks[11], (DEPTH, MLA_KV_RANK, MLA_HEADS * (MLA_NOPE + MLA_DV)), MLA_KV_RANK ** -0.5),
        "mla_proj": n(ks[12], (DEPTH, MLA_VW, D_MODEL), MLA_VW ** -0.5),
        "w_out": n(ks[13], (DEPTH, D_MODEL, D_MODEL), D_MODEL ** -0.5),
        "final_norm_g": 1.0 + n(ks[14], (D_MODEL,), 0.02),
    }


def reference(x, meta_tokens, norm_g, w_in, gla_gate_w, gla_gate_b, gla_norm_g, gla_proj,
              mla_q_norm_g, mla_w_uq, mla_kv_norm_g, mla_w_ukv, mla_proj, w_out, final_norm_g):
    B = x.shape[0]
    meta = jnp.broadcast_to(meta_tokens[None].astype(x.dtype), (B, N_META, D_MODEL))
    h = jnp.concatenate([meta, x], axis=1)
    L = h.shape[1]
    cuts = [int(c) for c in np.cumsum(SPLITS)[:-1]]
    for l in range(DEPTH):
        u = rms_norm(h, norm_g[l])
        proj = u @ w_in[l]
        (g_q, g_k, g_v, g_lr, g_z, m_cq, m_ckv, m_kr, m_z,
         gate_gla, gate_mla) = jnp.split(proj, cuts, axis=-1)

        q = g_q.reshape(B, L, GLA_HEADS, GLA_DK) * (GLA_DK ** -0.5)
        k = g_k.reshape(B, L, GLA_HEADS, GLA_DK)
        v = g_v.reshape(B, L, GLA_HEADS, GLA_DV)
        gk = jax.nn.log_sigmoid((g_lr @ gla_gate_w[l] + gla_gate_b[l]).astype(jnp.float32)) / GLA_GATE_NORMALIZER
        gk = gk.reshape(B, L, GLA_HEADS, GLA_DK)
        o_a = gla_chunked(q, k, v, gk)
        o_a = rms_norm(o_a, gla_norm_g[l]).reshape(B, L, GLA_VW).astype(h.dtype)
        y_a = (o_a * jax.nn.silu(g_z)) @ gla_proj[l]

        o_b = mla_attention(m_cq, m_ckv, m_kr, mla_q_norm_g[l], mla_w_uq[l],
                            mla_kv_norm_g[l], mla_w_ukv[l])
        y_b = (o_b * jax.nn.silu(m_z)) @ mla_proj[l]

        merged = jax.nn.sigmoid(gate_gla) * y_a + jax.nn.sigmoid(gate_mla) * y_b
        h = h + merged @ w_out[l]
    out = rms_norm(h, final_norm_g)
    return out[:, N_META:]
```

```python
import functools
import math

import jax
import jax.numpy as jnp
from jax import lax
from jax.experimental import pallas as pl
from jax.experimental.pallas import tpu as pltpu

F32 = jnp.float32
BF16 = jnp.bfloat16

D_MODEL = 1024
N_META = 16
EPS = 1e-6

GLA_HEADS = 4
GLA_DK = 128
GLA_DV = 256
GLA_GATE_RANK = 16
GLA_GATE_NORMALIZER = 16.0
GLA_CHUNK = 64
GLA_KW = GLA_HEADS * GLA_DK
GLA_VW = GLA_HEADS * GLA_DV

MLA_HEADS = 8
MLA_NOPE = 128
MLA_ROPE = 64
MLA_DV = 128
MLA_Q_RANK = 256
MLA_KV_RANK = 128
MLA_QK = MLA_NOPE + MLA_ROPE
MLA_VW = MLA_HEADS * MLA_DV
ROPE_BASE = 10000.0

SPLITS = (GLA_KW, GLA_KW, GLA_VW, GLA_GATE_RANK, GLA_VW,
          MLA_Q_RANK, MLA_KV_RANK, MLA_ROPE, MLA_VW, D_MODEL, D_MODEL)

LANES = 128
MLA_KPAD = 2 * LANES
W_MAIN = 2 * GLA_KW + 2 * GLA_VW + MLA_VW + 2 * D_MODEL
W_SMALL = MLA_Q_RANK + MLA_KV_RANK + 3 * LANES
VMEM_LIMIT = 56 * 1024 * 1024


def _rms(x, g):
    return x * lax.rsqrt(jnp.mean(x * x, axis=-1, keepdims=True) + EPS) * g


def _dot(a, b):
    return jnp.dot(a, b, preferred_element_type=F32)


def _dot_nt(a, b):
    return lax.dot_general(a, b, (((1,), (1,)), ((), ())), preferred_element_type=F32)


def _dot_tn(a, b):
    return lax.dot_general(a, b, (((0,), (0,)), ((), ())), preferred_element_type=F32)


def _inproj_kernel(x_ref, ng_ref, w1_ref, qg_ref, kvg_ref, wuq_ref, wukt_ref, cos_ref, sin_ref,
                   q_ref, k_ref, v_ref, sz_ref, smz_ref, sgg_ref, sgm_ref, glr_ref, qm_ref, km_ref):
    x = x_ref[0]
    u = _rms(x, ng_ref[...]).astype(BF16)

    def proj(lo, width):
        return _dot(u, w1_ref[:, lo:lo + width])

    o = 0
    q_ref[0] = (proj(o, GLA_KW) * (GLA_DK ** -0.5)).astype(BF16); o += GLA_KW
    k_ref[0] = proj(o, GLA_KW).astype(BF16); o += GLA_KW
    v_ref[0] = proj(o, GLA_VW).astype(BF16); o += GLA_VW
    sz_ref[0] = jax.nn.silu(proj(o, GLA_VW)).astype(BF16); o += GLA_VW
    smz_ref[0] = jax.nn.silu(proj(o, MLA_VW)).astype(BF16); o += MLA_VW
    sgg_ref[0] = jax.nn.sigmoid(proj(o, D_MODEL)).astype(BF16); o += D_MODEL
    sgm_ref[0] = jax.nn.sigmoid(proj(o, D_MODEL)).astype(BF16); o += D_MODEL

    cq = proj(o, MLA_Q_RANK); o += MLA_Q_RANK
    ckv = proj(o, MLA_KV_RANK); o += MLA_KV_RANK
    kr = proj(o, LANES); o += LANES
    krs = proj(o, LANES); o += LANES
    glr_ref[0] = proj(o, LANES).astype(BF16)

    cos = cos_ref[...]
    sin = sin_ref[...]
    km_ref[0, :, 0:LANES] = _rms(ckv, kvg_ref[...]).astype(BF16)
    km_ref[0, :, LANES:MLA_KPAD] = (kr * cos + krs * sin).astype(BF16)

    cqn = _rms(cq, qg_ref[...]).astype(BF16)
    scale = MLA_QK ** -0.5
    nope_w = MLA_HEADS * MLA_NOPE
    rope_w = MLA_HEADS * MLA_ROPE
    for h in range(MLA_HEADS):
        qn = _dot(cqn, wuq_ref[:, h * MLA_NOPE:(h + 1) * MLA_NOPE]).astype(BF16)
        qm_ref[0, h, :, 0:LANES] = (_dot(qn, wukt_ref[h]) * scale).astype(BF16)
    lane = lax.broadcasted_iota(jnp.int32, (x.shape[0], LANES), 1)
    for g in range(MLA_HEADS // 2):
        a = _dot(cqn, wuq_ref[:, nope_w + g * LANES:nope_w + (g + 1) * LANES])
        b = _dot(cqn, wuq_ref[:, nope_w + rope_w + g * LANES:nope_w + rope_w + (g + 1) * LANES])
        r = (a * cos + b * sin) * scale
        qm_ref[0, 2 * g, :, LANES:MLA_KPAD] = jnp.where(lane < MLA_ROPE, r, 0.0).astype(BF16)
        qm_ref[0, 2 * g + 1, :, LANES:MLA_KPAD] = jnp.where(lane >= MLA_ROPE, r, 0.0).astype(BF16)


def _inproj(x, ng, w1, qg, kvg, wuq, wukt, cos, sin, tm):
    B, S, _ = x.shape
    grid = (B, S // tm)
    tok = lambda w: pl.BlockSpec((1, tm, w), lambda b, i: (b, i, 0))
    full = lambda a: pl.BlockSpec(a.shape, lambda b, i: (0,) * a.ndim)
    out_shapes = [
        jax.ShapeDtypeStruct((B, S, GLA_KW), BF16),
        jax.ShapeDtypeStruct((B, S, GLA_KW), BF16),
        jax.ShapeDtypeStruct((B, S, GLA_VW), BF16),
        jax.ShapeDtypeStruct((B, S, GLA_VW), BF16),
        jax.ShapeDtypeStruct((B, S, MLA_VW), BF16),
        jax.ShapeDtypeStruct((B, S, D_MODEL), BF16),
        jax.ShapeDtypeStruct((B, S, D_MODEL), BF16),
        jax.ShapeDtypeStruct((B, S, LANES), BF16),
        jax.ShapeDtypeStruct((B, MLA_HEADS, S, MLA_KPAD), BF16),
        jax.ShapeDtypeStruct((B, S, MLA_KPAD), BF16),
    ]
    out_specs = [tok(GLA_KW), tok(GLA_KW), tok(GLA_VW), tok(GLA_VW), tok(MLA_VW),
                 tok(D_MODEL), tok(D_MODEL), tok(LANES),
                 pl.BlockSpec((1, MLA_HEADS, tm, MLA_KPAD), lambda b, i: (b, 0, i, 0)),
                 tok(MLA_KPAD)]
    tab = pl.BlockSpec((tm, LANES), lambda b, i: (i, 0))
    return pl.pallas_call(
        _inproj_kernel,
        grid=grid,
        in_specs=[tok(D_MODEL), full(ng), full(w1), full(qg), full(kvg), full(wuq), full(wukt), tab, tab],
        out_specs=out_specs,
        out_shape=out_shapes,
        compiler_params=pltpu.CompilerParams(
            dimension_semantics=("parallel", "parallel"), vmem_limit_bytes=VMEM_LIMIT),
        name="inproj",
    )(x, ng, w1, qg, kvg, wuq, wukt, cos, sin)


def _gla_gates(glr, gw_ref, gb_ref, tri):
    g = jax.nn.log_sigmoid(_dot(glr, gw_ref[...]) + gb_ref[...]) / GLA_GATE_NORMALIZER
    b = jnp.dot(tri, g, preferred_element_type=F32, precision=lax.Precision.HIGHEST)
    return b, b[-1:, :]


def _gla_kernel(q_ref, k_ref, v_ref, glr_ref, sz_ref, km_ref, vm_ref, glrm_ref,
                gw_ref, gb_ref, gng_ref, o_ref, st_ref, *, n_chunks):
    C = GLA_CHUNK

    @pl.when(pl.program_id(1) == 0)
    def _init_state():
        n = N_META
        row = lax.broadcasted_iota(jnp.int32, (n, n), 0)
        col = lax.broadcasted_iota(jnp.int32, (n, n), 1)
        tri = (col <= row).astype(F32)
        b, b_last = _gla_gates(glrm_ref[0], gw_ref, gb_ref, tri)
        kl = (km_ref[0].astype(F32) * jnp.exp(b_last - b)).astype(BF16)
        for h in range(GLA_HEADS):
            st_ref[h] = _dot_tn(vm_ref[0, :, h * GLA_DV:(h + 1) * GLA_DV],
                                kl[:, h * GLA_DK:(h + 1) * GLA_DK])

    row = lax.broadcasted_iota(jnp.int32, (C, C), 0)
    col = lax.broadcasted_iota(jnp.int32, (C, C), 1)
    causal = col <= row
    tri = causal.astype(F32)
    gng = gng_ref[...]

    def chunk(c, carry):
        r0 = pl.multiple_of(c * C, C)
        rows = pl.ds(r0, C)
        b, b_last = _gla_gates(glr_ref[0, rows, :], gw_ref, gb_ref, tri)
        q = q_ref[0, rows, :].astype(F32)
        k = k_ref[0, rows, :].astype(F32)
        qe = (q * jnp.exp(b)).astype(BF16)
        ke = (k * jnp.exp(-b)).astype(BF16)
        kl = (k * jnp.exp(b_last - b)).astype(BF16)
        decay = jnp.exp(b_last)
        for h in range(GLA_HEADS):
            ks = slice(h * GLA_DK, (h + 1) * GLA_DK)
            vs = slice(h * GLA_DV, (h + 1) * GLA_DV)
            v = v_ref[0, rows, vs]
            st = st_ref[h]
            a = jnp.where(causal, _dot_nt(qe[:, ks], ke[:, ks]), 0.0).astype(BF16)
            o = _dot(a, v) + _dot_nt(qe[:, ks], st.astype(BF16))
            st_ref[h] = st * decay[:, ks] + _dot_tn(v, kl[:, ks])
            o = _rms(o, gng) * sz_ref[0, rows, vs].astype(F32)
            o_ref[0, rows, vs] = o.astype(BF16)
        return carry

    lax.fori_loop(0, n_chunks, chunk, 0)


def _gla(q, k, v, glr, sz, km, vm, glrm, gw, gb, gng, tc):
    B, S, _ = q.shape
    grid = (B, S // tc)
    tok = lambda w: pl.BlockSpec((1, tc, w), lambda b, i: (b, i, 0))
    full = lambda a: pl.BlockSpec(a.shape, lambda b, i: (0,) * a.ndim)
    return pl.pallas_call(
        functools.partial(_gla_kernel, n_chunks=tc // GLA_CHUNK),
        grid=grid,
        in_specs=[tok(GLA_KW), tok(GLA_KW), tok(GLA_VW), tok(LANES), tok(GLA_VW),
                  full(km), full(vm), full(glrm), full(gw), full(gb), full(gng)],
        out_specs=tok(GLA_VW),
        out_shape=jax.ShapeDtypeStruct((B, S, GLA_VW), BF16),
        scratch_shapes=[pltpu.VMEM((GLA_HEADS, GLA_DV, GLA_DK), F32)],
        compiler_params=pltpu.CompilerParams(
            dimension_semantics=("parallel", "arbitrary"), vmem_limit_bytes=VMEM_LIMIT),
        name="gla",
    )(q, k, v, glr, sz, km, vm, glrm, gw, gb, gng)


def _mla_kernel(qm_ref, km_ref, kmeta_ref, wuv_ref, smz_ref, o_ref, m_ref, l_ref, acc_ref, *, tq):
    i = pl.program_id(1)
    rows = MLA_HEADS * tq
    q = qm_ref[0].reshape(rows, MLA_KPAD)

    kmeta = kmeta_ref[0]
    s = _dot_nt(q, kmeta)
    m0 = jnp.max(s, axis=-1, keepdims=True)
    p = jnp.exp(s - m0)
    m_ref[...] = m0
    l_ref[...] = jnp.sum(p, axis=-1, keepdims=True)
    acc_ref[...] = _dot(p.astype(BF16), kmeta[:, 0:MLA_KV_RANK])

    def update(kb, mask):
        s = _dot_nt(q, kb)
        if mask is not None:
            s = jnp.where(mask, s, -jnp.inf)
        m_old = m_ref[...]
        m_new = jnp.maximum(m_old, jnp.max(s, axis=-1, keepdims=True))
        alpha = jnp.exp(m_old - m_new)
        p = jnp.exp(s - m_new)
        l_ref[...] = alpha * l_ref[...] + jnp.sum(p, axis=-1, keepdims=True)
        acc_ref[...] = alpha * acc_ref[...] + _dot(p.astype(BF16), kb[:, 0:MLA_KV_RANK])
        m_ref[...] = m_new

    def full_block(j, carry):
        update(km_ref[0, pl.ds(pl.multiple_of(j * tq, tq), tq), :], None)
        return carry

    lax.fori_loop(0, i, full_block, 0)

    qtok = lax.broadcasted_iota(jnp.int32, (rows, tq), 0) & (tq - 1)
    ktok = lax.broadcasted_iota(jnp.int32, (rows, tq), 1)
    update(km_ref[0, pl.ds(pl.multiple_of(i * tq, tq), tq), :], ktok <= qtok)

    o_lat = (acc_ref[...] / l_ref[...]).astype(BF16)
    for h in range(MLA_HEADS):
        vs = slice(h * MLA_DV, (h + 1) * MLA_DV)
        o = _dot(o_lat[h * tq:(h + 1) * tq], wuv_ref[h])
        o_ref[0, :, vs] = (o * smz_ref[0, :, vs].astype(F32)).astype(BF16)


def _mla(qm, km, kmeta, wuv, smz, tq):
    B, H, S, _ = qm.shape
    grid = (B, S // tq)
    full = lambda a: pl.BlockSpec(a.shape, lambda b, i: (0,) * a.ndim)
    rows = H * tq
    return pl.pallas_call(
        functools.partial(_mla_kernel, tq=tq),
        grid=grid,
        in_specs=[pl.BlockSpec((1, H, tq, MLA_KPAD), lambda b, i: (b, 0, i, 0)),
                  pl.BlockSpec((1, S, MLA_KPAD), lambda b, i: (b, 0, 0)),
                  full(kmeta), full(wuv),
                  pl.BlockSpec((1, tq, MLA_VW), lambda b, i: (b, i, 0))],
        out_specs=pl.BlockSpec((1, tq, MLA_VW), lambda b, i: (b, i, 0)),
        out_shape=jax.ShapeDtypeStruct((B, S, MLA_VW), BF16),
        scratch_shapes=[pltpu.VMEM((rows, 1), F32), pltpu.VMEM((rows, 1), F32),
                        pltpu.VMEM((rows, MLA_KV_RANK), F32)],
        compiler_params=pltpu.CompilerParams(
            dimension_semantics=("parallel", "arbitrary"), vmem_limit_bytes=VMEM_LIMIT),
        name="mla",
    )(qm, km, kmeta, wuv, smz)


def _outproj_kernel(ya_ref, yb_ref, sgg_ref, sgm_ref, x_ref, gp_ref, mp_ref, wo_ref, fng_ref, o_ref):
    y_a = _dot(ya_ref[0], gp_ref[...])
    y_b = _dot(yb_ref[0], mp_ref[...])
    merged = sgg_ref[0].astype(F32) * y_a + sgm_ref[0].astype(F32) * y_b
    h = x_ref[0] + _dot(merged.astype(BF16), wo_ref[...])
    o_ref[0] = _rms(h, fng_ref[...])


def _outproj(ya, yb, sgg, sgm, x, gp, mp, wo, fng, tm):
    B, S, _ = x.shape
    grid = (B, S // tm)
    tok = pl.BlockSpec((1, tm, D_MODEL), lambda b, i: (b, i, 0))
    full = lambda a: pl.BlockSpec(a.shape, lambda b, i: (0,) * a.ndim)
    return pl.pallas_call(
        _outproj_kernel,
        grid=grid,
        in_specs=[tok, tok, tok, tok, tok, full(gp), full(mp), full(wo), full(fng)],
        out_specs=tok,
        out_shape=jax.ShapeDtypeStruct((B, S, D_MODEL), F32),
        compiler_params=pltpu.CompilerParams(
            dimension_semantics=("parallel", "parallel"), vmem_limit_bytes=VMEM_LIMIT),
        name="outproj",
    )(ya, yb, sgg, sgm, x, gp, mp, wo, fng)


def _swap_halves(w):
    half = w.shape[-1] // 2
    return jnp.concatenate([w[..., half:], w[..., :half]], axis=-1)


def _prep_weights(w_in, gla_gate_w, mla_w_uq, mla_w_ukv):
    cuts = [0]
    for s in SPLITS:
        cuts.append(cuts[-1] + s)
    (g_q, g_k, g_v, g_lr, g_z, m_cq, m_ckv, m_kr, m_z, gate_gla, gate_mla) = [
        w_in[:, cuts[n]:cuts[n + 1]] for n in range(len(SPLITS))]
    m_krs = _swap_halves(m_kr)
    pad = jnp.zeros((D_MODEL, LANES - GLA_GATE_RANK), w_in.dtype)
    w1 = jnp.concatenate([g_q, g_k, g_v, g_z, m_z, gate_gla, gate_mla,
                          m_cq, m_ckv, m_kr, m_kr, m_krs, m_krs, g_lr, pad], axis=1).astype(BF16)
    assert w1.shape == (D_MODEL, W_MAIN + W_SMALL)

    uq = mla_w_uq.reshape(MLA_Q_RANK, MLA_HEADS, MLA_QK)
    uq_nope = uq[:, :, :MLA_NOPE].reshape(MLA_Q_RANK, MLA_HEADS * MLA_NOPE)
    uq_rope = uq[:, :, MLA_NOPE:]
    wuq = jnp.concatenate([uq_nope, uq_rope.reshape(MLA_Q_RANK, -1),
                           _swap_halves(uq_rope).reshape(MLA_Q_RANK, -1)], axis=1).astype(BF16)

    ukv = mla_w_ukv.reshape(MLA_KV_RANK, MLA_HEADS, MLA_NOPE + MLA_DV)
    wukt = jnp.transpose(ukv[:, :, :MLA_NOPE], (1, 2, 0)).astype(BF16)
    wuv = jnp.transpose(ukv[:, :, MLA_NOPE:], (1, 0, 2)).astype(BF16)

    gw = jnp.concatenate([gla_gate_w, jnp.zeros((LANES - GLA_GATE_RANK, GLA_KW), gla_gate_w.dtype)],
                         axis=0).astype(BF16)
    return w1, wuq, wukt, wuv, gw


def _rope_tables(n):
    inv = 1.0 / (ROPE_BASE ** (jnp.arange(0, MLA_ROPE, 2, dtype=F32) / MLA_ROPE))
    ang = jnp.arange(n, dtype=F32)[:, None] * inv[None, :]
    cos, sin = jnp.cos(ang), jnp.sin(ang)
    cos = jnp.concatenate([cos, cos, cos, cos], axis=1)
    sin = jnp.concatenate([-sin, sin, -sin, sin], axis=1)
    return cos, sin


def kernel(x, meta_tokens, norm_g, w_in, gla_gate_w, gla_gate_b, gla_norm_g, gla_proj,
           mla_q_norm_g, mla_w_uq, mla_kv_norm_g, mla_w_ukv, mla_proj, w_out, final_norm_g):
    B, S, D = x.shape
    assert D == D_MODEL and norm_g.shape[0] == 1 and meta_tokens.shape == (N_META, D_MODEL)

    w1, wuq, wukt, wuv, gw = _prep_weights(w_in[0], gla_gate_w[0], mla_w_uq[0], mla_w_ukv[0])
    row = lambda a: a.reshape(1, -1).astype(F32)
    ng, qg, kvg = row(norm_g[0]), row(mla_q_norm_g[0]), row(mla_kv_norm_g[0])
    gb, gng, fng = row(gla_gate_b[0]), row(gla_norm_g[0]), row(final_norm_g)
    cos, sin = _rope_tables(N_META + S)

    proj_args = (ng, w1, qg, kvg, wuq, wukt)
    meta = _inproj(meta_tokens[None].astype(F32), *proj_args, cos[:N_META], sin[:N_META], tm=N_META)
    _, k_m, v_m, _, _, _, _, glr_m, _, kmeta = meta
    q, k, v, sz, smz, sgg, sgm, glr, qm, km = _inproj(
        x, *proj_args, cos[N_META:], sin[N_META:], tm=256)

    ya = _gla(q, k, v, glr, sz, k_m, v_m, glr_m, gw, gb, gng, tc=512)
    yb = _mla(qm, km, kmeta, wuv, smz, tq=128)
    return _outproj(ya, yb, sgg, sgm, x, gla_proj[0].astype(BF16), mla_proj[0].astype(BF16),
                    w_out[0].astype(BF16), fng, tm=512)
```

```python
import functools
import math

import jax
import jax.numpy as jnp
from jax import lax
from jax.experimental import pallas as pl
from jax.experimental.pallas import tpu as pltpu

F32 = jnp.float32
BF16 = jnp.bfloat16

D_MODEL = 1024
N_META = 16
EPS = 1e-6

GLA_HEADS = 4
GLA_DK = 128
GLA_DV = 256
GLA_GATE_RANK = 16
GLA_GATE_NORMALIZER = 16.0
GLA_CHUNK = 64
GLA_KW = GLA_HEADS * GLA_DK
GLA_VW = GLA_HEADS * GLA_DV

MLA_HEADS = 8
MLA_NOPE = 128
MLA_ROPE = 64
MLA_DV = 128
MLA_Q_RANK = 256
MLA_KV_RANK = 128
MLA_QK = MLA_NOPE + MLA_ROPE
MLA_VW = MLA_HEADS * MLA_DV
ROPE_BASE = 10000.0

SPLITS = (GLA_KW, GLA_KW, GLA_VW, GLA_GATE_RANK, GLA_VW,
          MLA_Q_RANK, MLA_KV_RANK, MLA_ROPE, MLA_VW, D_MODEL, D_MODEL)

LANES = 128
MLA_KPAD = 2 * LANES
W_MAIN = 2 * GLA_KW + 2 * GLA_VW + MLA_VW + 2 * D_MODEL
W_SMALL = MLA_Q_RANK + MLA_KV_RANK + 3 * LANES
VMEM_LIMIT = 56 * 1024 * 1024


def _rms(x, g):
    return x * lax.rsqrt(jnp.mean(x * x, axis=-1, keepdims=True) + EPS) * g


def _dot(a, b):
    return jnp.dot(a, b, preferred_element_type=F32)


def _dot_nt(a, b):
    return lax.dot_general(a, b, (((1,), (1,)), ((), ())), preferred_element_type=F32)


def _dot_tn(a, b):
    return lax.dot_general(a, b, (((0,), (0,)), ((), ())), preferred_element_type=F32)


def _inproj_kernel(x_ref, ng_ref, w1_ref, qg_ref, kvg_ref, wuq_ref, wukt_ref, cos_ref, sin_ref,
                   q_ref, k_ref, v_ref, sz_ref, smz_ref, sgg_ref, sgm_ref, glr_ref, qm_ref, km_ref):
    x = x_ref[0]
    u = _rms(x, ng_ref[...]).astype(BF16)

    def proj(lo, width):
        return _dot(u, w1_ref[:, lo:lo + width])

    o = 0
    q_ref[0] = (proj(o, GLA_KW) * (GLA_DK ** -0.5)).astype(BF16); o += GLA_KW
    k_ref[0] = proj(o, GLA_KW).astype(BF16); o += GLA_KW
    v_ref[0] = proj(o, GLA_VW).astype(BF16); o += GLA_VW
    sz_ref[0] = jax.nn.silu(proj(o, GLA_VW)).astype(BF16); o += GLA_VW
    smz_ref[0] = jax.nn.silu(proj(o, MLA_VW)).astype(BF16); o += MLA_VW
    sgg_ref[0] = jax.nn.sigmoid(proj(o, D_MODEL)).astype(BF16); o += D_MODEL
    sgm_ref[0] = jax.nn.sigmoid(proj(o, D_MODEL)).astype(BF16); o += D_MODEL

    cq = proj(o, MLA_Q_RANK); o += MLA_Q_RANK
    ckv = proj(o, MLA_KV_RANK); o += MLA_KV_RANK
    kr = proj(o, LANES); o += LANES
    krs = proj(o, LANES); o += LANES
    glr_ref[0] = proj(o, LANES).astype(BF16)

    cos = cos_ref[...]
    sin = sin_ref[...]
    km_ref[0, :, 0:LANES] = _rms(ckv, kvg_ref[...]).astype(BF16)
    km_ref[0, :, LANES:MLA_KPAD] = (kr * cos + krs * sin).astype(BF16)

    cqn = _rms(cq, qg_ref[...]).astype(BF16)
    scale = MLA_QK ** -0.5 * math.log2(math.e)
    nope_w = MLA_HEADS * MLA_NOPE
    rope_w = MLA_HEADS * MLA_ROPE
    for h in range(MLA_HEADS):
        qn = _dot(cqn, wuq_ref[:, h * MLA_NOPE:(h + 1) * MLA_NOPE]).astype(BF16)
        qm_ref[0, h, :, 0:LANES] = (_dot(qn, wukt_ref[h]) * scale).astype(BF16)
    lane = lax.broadcasted_iota(jnp.int32, (x.shape[0], LANES), 1)
    for g in range(MLA_HEADS // 2):
        a = _dot(cqn, wuq_ref[:, nope_w + g * LANES:nope_w + (g + 1) * LANES])
        b = _dot(cqn, wuq_ref[:, nope_w + rope_w + g * LANES:nope_w + rope_w + (g + 1) * LANES])
        r = (a * cos + b * sin) * scale
        qm_ref[0, 2 * g, :, LANES:MLA_KPAD] = jnp.where(lane < MLA_ROPE, r, 0.0).astype(BF16)
        qm_ref[0, 2 * g + 1, :, LANES:MLA_KPAD] = jnp.where(lane >= MLA_ROPE, r, 0.0).astype(BF16)


def _inproj(x, ng, w1, qg, kvg, wuq, wukt, cos, sin, tm):
    B, S, _ = x.shape
    grid = (B, S // tm)
    tok = lambda w: pl.BlockSpec((1, tm, w), lambda b, i: (b, i, 0))
    full = lambda a: pl.BlockSpec(a.shape, lambda b, i: (0,) * a.ndim)
    out_shapes = [
        jax.ShapeDtypeStruct((B, S, GLA_KW), BF16),
        jax.ShapeDtypeStruct((B, S, GLA_KW), BF16),
        jax.ShapeDtypeStruct((B, S, GLA_VW), BF16),
        jax.ShapeDtypeStruct((B, S, GLA_VW), BF16),
        jax.ShapeDtypeStruct((B, S, MLA_VW), BF16),
        jax.ShapeDtypeStruct((B, S, D_MODEL), BF16),
        jax.ShapeDtypeStruct((B, S, D_MODEL), BF16),
        jax.ShapeDtypeStruct((B, S, LANES), BF16),
        jax.ShapeDtypeStruct((B, MLA_HEADS, S, MLA_KPAD), BF16),
        jax.ShapeDtypeStruct((B, S, MLA_KPAD), BF16),
    ]
    out_specs = [tok(GLA_KW), tok(GLA_KW), tok(GLA_VW), tok(GLA_VW), tok(MLA_VW),
                 tok(D_MODEL), tok(D_MODEL), tok(LANES),
                 pl.BlockSpec((1, MLA_HEADS, tm, MLA_KPAD), lambda b, i: (b, 0, i, 0)),
                 tok(MLA_KPAD)]
    tab = pl.BlockSpec((tm, LANES), lambda b, i: (i, 0))
    return pl.pallas_call(
        _inproj_kernel,
        grid=grid,
        in_specs=[tok(D_MODEL), full(ng), full(w1), full(qg), full(kvg), full(wuq), full(wukt), tab, tab],
        out_specs=out_specs,
        out_shape=out_shapes,
        compiler_params=pltpu.CompilerParams(
            dimension_semantics=("parallel", "parallel"), vmem_limit_bytes=VMEM_LIMIT),
        name="inproj",
    )(x, ng, w1, qg, kvg, wuq, wukt, cos, sin)


def _gla_gates(glr, gw_ref, gb_ref, tri):
    g = jax.nn.log_sigmoid(_dot(glr, gw_ref[...]) + gb_ref[...]) / GLA_GATE_NORMALIZER
    b = jnp.dot(tri, g, preferred_element_type=F32, precision=lax.Precision.HIGHEST)
    return b, b[-1:, :]


def _gla_kernel(q_ref, k_ref, v_ref, glr_ref, sz_ref, km_ref, vm_ref, glrm_ref,
                gw_ref, gb_ref, gng_ref, o_ref, st_ref, *, n_chunks):
    C = GLA_CHUNK

    @pl.when(pl.program_id(1) == 0)
    def _init_state():
        n = N_META
        row = lax.broadcasted_iota(jnp.int32, (n, n), 0)
        col = lax.broadcasted_iota(jnp.int32, (n, n), 1)
        tri = (col <= row).astype(F32)
        b, b_last = _gla_gates(glrm_ref[0], gw_ref, gb_ref, tri)
        kl = (km_ref[0].astype(F32) * jnp.exp(b_last - b)).astype(BF16)
        for h in range(GLA_HEADS):
            st_ref[h] = _dot_tn(vm_ref[0, :, h * GLA_DV:(h + 1) * GLA_DV],
                                kl[:, h * GLA_DK:(h + 1) * GLA_DK])

    row = lax.broadcasted_iota(jnp.int32, (C, C), 0)
    col = lax.broadcasted_iota(jnp.int32, (C, C), 1)
    causal = col <= row
    tri = causal.astype(F32)
    gng = gng_ref[...]

    def chunk(c, carry):
        r0 = pl.multiple_of(c * C, C)
        rows = pl.ds(r0, C)
        b, b_last = _gla_gates(glr_ref[0, rows, :], gw_ref, gb_ref, tri)
        q = q_ref[0, rows, :].astype(F32)
        k = k_ref[0, rows, :].astype(F32)
        qe = (q * jnp.exp(b)).astype(BF16)
        ke = (k * jnp.exp(-b)).astype(BF16)
        kl = (k * jnp.exp(b_last - b)).astype(BF16)
        decay = jnp.exp(b_last)
        for h in range(GLA_HEADS):
            ks = slice(h * GLA_DK, (h + 1) * GLA_DK)
            vs = slice(h * GLA_DV, (h + 1) * GLA_DV)
            v = v_ref[0, rows, vs]
            st = st_ref[h]
            a = jnp.where(causal, _dot_nt(qe[:, ks], ke[:, ks]), 0.0).astype(BF16)
            o = _dot(a, v) + _dot_nt(qe[:, ks], st.astype(BF16))
            st_ref[h] = st * decay[:, ks] + _dot_tn(v, kl[:, ks])
            o = _rms(o, gng) * sz_ref[0, rows, vs].astype(F32)
            o_ref[0, rows, vs] = o.astype(BF16)
        return carry

    lax.fori_loop(0, n_chunks, chunk, 0)


def _gla(q, k, v, glr, sz, km, vm, glrm, gw, gb, gng, tc):
    B, S, _ = q.shape
    grid = (B, S // tc)
    tok = lambda w: pl.BlockSpec((1, tc, w), lambda b, i: (b, i, 0))
    full = lambda a: pl.BlockSpec(a.shape, lambda b, i: (0,) * a.ndim)
    return pl.pallas_call(
        functools.partial(_gla_kernel, n_chunks=tc // GLA_CHUNK),
        grid=grid,
        in_specs=[tok(GLA_KW), tok(GLA_KW), tok(GLA_VW), tok(LANES), tok(GLA_VW),
                  full(km), full(vm), full(glrm), full(gw), full(gb), full(gng)],
        out_specs=tok(GLA_VW),
        out_shape=jax.ShapeDtypeStruct((B, S, GLA_VW), BF16),
        scratch_shapes=[pltpu.VMEM((GLA_HEADS, GLA_DV, GLA_DK), F32)],
        compiler_params=pltpu.CompilerParams(
            dimension_semantics=("parallel", "arbitrary"), vmem_limit_bytes=VMEM_LIMIT),
        name="gla",
    )(q, k, v, glr, sz, km, vm, glrm, gw, gb, gng)


def _mla_kernel(qm_ref, km_ref, kmeta_ref, wuv_ref, smz_ref, o_ref, m_ref, l_ref, acc_ref, *, tq, tk):
    i = pl.program_id(1)
    cols = MLA_HEADS * tq
    gw = MLA_KPAD
    q = qm_ref[0].reshape(cols, MLA_KPAD)

    kmeta = kmeta_ref[0]
    s = _dot_nt(kmeta, q)
    m0 = jnp.max(s, axis=0, keepdims=True)
    p = jnp.exp2(s - m0)
    m_ref[...] = m0
    l_ref[...] = jnp.sum(p, axis=0, keepdims=True)
    acc_ref[...] = _dot_tn(kmeta[:, 0:MLA_KV_RANK], p.astype(BF16))

    def update(kb, mask):
        groups = [slice(c0, c0 + gw) for c0 in range(0, cols, gw)]
        s_next = _dot_nt(kb, q[groups[0]])
        for g, cs in enumerate(groups):
            s = s_next
            if g + 1 < len(groups):
                s_next = _dot_nt(kb, q[groups[g + 1]])
            if mask is not None:
                s = jnp.where(mask[:, cs], s, -jnp.inf)
            m_old = m_ref[:, cs]
            m_new = jnp.maximum(m_old, jnp.max(s, axis=0, keepdims=True))
            alpha = jnp.exp2(m_old - m_new)
            p = jnp.exp2(s - m_new)
            l_ref[:, cs] = alpha * l_ref[:, cs] + jnp.sum(p, axis=0, keepdims=True)
            acc_ref[:, cs] = alpha * acc_ref[:, cs] + _dot_tn(kb[:, 0:MLA_KV_RANK], p.astype(BF16))
            m_ref[:, cs] = m_new

    def full_block(j, carry):
        update(km_ref[0, pl.ds(pl.multiple_of(j * tk, tk), tk), :], None)
        return carry

    n_full = (i * tq) // tk
    lax.fori_loop(0, n_full, full_block, 0)

    k0 = pl.multiple_of(n_full * tk, tk)
    ktok = k0 + lax.broadcasted_iota(jnp.int32, (tk, cols), 0)
    qtok = i * tq + (lax.broadcasted_iota(jnp.int32, (tk, cols), 1) & (tq - 1))
    update(km_ref[0, pl.ds(k0, tk), :], ktok <= qtok)

    o_lat = (acc_ref[...] / l_ref[...]).astype(BF16)
    for h in range(MLA_HEADS):
        vs = slice(h * MLA_DV, (h + 1) * MLA_DV)
        o = _dot_tn(o_lat[:, h * tq:(h + 1) * tq], wuv_ref[h])
        o_ref[0, :, vs] = (o * smz_ref[0, :, vs].astype(F32)).astype(BF16)


def _mla(qm, km, kmeta, wuv, smz, tq, tk):
    B, H, S, _ = qm.shape
    assert tq & (tq - 1) == 0 and S % tk == 0 and tk % tq == 0
    grid = (B, S // tq)
    full = lambda a: pl.BlockSpec(a.shape, lambda b, i: (0,) * a.ndim)
    rows = H * tq
    return pl.pallas_call(
        functools.partial(_mla_kernel, tq=tq, tk=tk),
        grid=grid,
        in_specs=[pl.BlockSpec((1, H, tq, MLA_KPAD), lambda b, i: (b, 0, i, 0)),
                  pl.BlockSpec((1, S, MLA_KPAD), lambda b, i: (b, 0, 0)),
                  full(kmeta), full(wuv),
                  pl.BlockSpec((1, tq, MLA_VW), lambda b, i: (b, i, 0))],
        out_specs=pl.BlockSpec((1, tq, MLA_VW), lambda b, i: (b, i, 0)),
        out_shape=jax.ShapeDtypeStruct((B, S, MLA_VW), BF16),
        scratch_shapes=[pltpu.VMEM((1, rows), F32), pltpu.VMEM((1, rows), F32),
                        pltpu.VMEM((MLA_KV_RANK, rows), F32)],
        compiler_params=pltpu.CompilerParams(
            dimension_semantics=("parallel", "arbitrary"), vmem_limit_bytes=VMEM_LIMIT),
        name="mla",
    )(qm, km, kmeta, wuv, smz)


def _outproj_kernel(ya_ref, yb_ref, sgg_ref, sgm_ref, x_ref, gp_ref, mp_ref, wo_ref, fng_ref, o_ref):
    y_a = _dot(ya_ref[0], gp_ref[...])
    y_b = _dot(yb_ref[0], mp_ref[...])
    merged = sgg_ref[0].astype(F32) * y_a + sgm_ref[0].astype(F32) * y_b
    h = x_ref[0] + _dot(merged.astype(BF16), wo_ref[...])
    o_ref[0] = _rms(h, fng_ref[...])


def _outproj(ya, yb, sgg, sgm, x, gp, mp, wo, fng, tm):
    B, S, _ = x.shape
    grid = (B, S // tm)
    tok = pl.BlockSpec((1, tm, D_MODEL), lambda b, i: (b, i, 0))
    full = lambda a: pl.BlockSpec(a.shape, lambda b, i: (0,) * a.ndim)
    return pl.pallas_call(
        _outproj_kernel,
        grid=grid,
        in_specs=[tok, tok, tok, tok, tok, full(gp), full(mp), full(wo), full(fng)],
        out_specs=tok,
        out_shape=jax.ShapeDtypeStruct((B, S, D_MODEL), F32),
        compiler_params=pltpu.CompilerParams(
            dimension_semantics=("parallel", "parallel"), vmem_limit_bytes=VMEM_LIMIT),
        name="outproj",
    )(ya, yb, sgg, sgm, x, gp, mp, wo, fng)


def _swap_halves(w):
    half = w.shape[-1] // 2
    return jnp.concatenate([w[..., half:], w[..., :half]], axis=-1)


def _prep_weights(w_in, gla_gate_w, mla_w_uq, mla_w_ukv):
    cuts = [0]
    for s in SPLITS:
        cuts.append(cuts[-1] + s)
    (g_q, g_k, g_v, g_lr, g_z, m_cq, m_ckv, m_kr, m_z, gate_gla, gate_mla) = [
        w_in[:, cuts[n]:cuts[n + 1]] for n in range(len(SPLITS))]
    m_krs = _swap_halves(m_kr)
    pad = jnp.zeros((D_MODEL, LANES - GLA_GATE_RANK), w_in.dtype)
    w1 = jnp.concatenate([g_q, g_k, g_v, g_z, m_z, gate_gla, gate_mla,
                          m_cq, m_ckv, m_kr, m_kr, m_krs, m_krs, g_lr, pad], axis=1).astype(BF16)
    assert w1.shape == (D_MODEL, W_MAIN + W_SMALL)

    uq = mla_w_uq.reshape(MLA_Q_RANK, MLA_HEADS, MLA_QK)
    uq_nope = uq[:, :, :MLA_NOPE].reshape(MLA_Q_RANK, MLA_HEADS * MLA_NOPE)
    uq_rope = uq[:, :, MLA_NOPE:]
    wuq = jnp.concatenate([uq_nope, uq_rope.reshape(MLA_Q_RANK, -1),
                           _swap_halves(uq_rope).reshape(MLA_Q_RANK, -1)], axis=1).astype(BF16)

    ukv = mla_w_ukv.reshape(MLA_KV_RANK, MLA_HEADS, MLA_NOPE + MLA_DV)
    wukt = jnp.transpose(ukv[:, :, :MLA_NOPE], (1, 2, 0)).astype(BF16)
    wuv = jnp.transpose(ukv[:, :, MLA_NOPE:], (1, 0, 2)).astype(BF16)

    gw = jnp.concatenate([gla_gate_w, jnp.zeros((LANES - GLA_GATE_RANK, GLA_KW), gla_gate_w.dtype)],
                         axis=0).astype(BF16)
    return w1, wuq, wukt, wuv, gw


def _rope_tables(n):
    inv = 1.0 / (ROPE_BASE ** (jnp.arange(0, MLA_ROPE, 2, dtype=F32) / MLA_ROPE))
    ang = jnp.arange(n, dtype=F32)[:, None] * inv[None, :]
    cos, sin = jnp.cos(ang), jnp.sin(ang)
    cos = jnp.concatenate([cos, cos, cos, cos], axis=1)
    sin = jnp.concatenate([-sin, sin, -sin, sin], axis=1)
    return cos, sin


def kernel(x, meta_tokens, norm_g, w_in, gla_gate_w, gla_gate_b, gla_norm_g, gla_proj,
           mla_q_norm_g, mla_w_uq, mla_kv_norm_g, mla_w_ukv, mla_proj, w_out, final_norm_g):
    B, S, D = x.shape
    assert D == D_MODEL and norm_g.shape[0] == 1 and meta_tokens.shape == (N_META, D_MODEL)

    w1, wuq, wukt, wuv, gw = _prep_weights(w_in[0], gla_gate_w[0], mla_w_uq[0], mla_w_ukv[0])
    row = lambda a: a.reshape(1, -1).astype(F32)
    ng, qg, kvg = row(norm_g[0]), row(mla_q_norm_g[0]), row(mla_kv_norm_g[0])
    gb, gng, fng = row(gla_gate_b[0]), row(gla_norm_g[0]), row(final_norm_g)
    cos, sin = _rope_tables(N_META + S)

    proj_args = (ng, w1, qg, kvg, wuq, wukt)
    meta = _inproj(meta_tokens[None].astype(F32), *proj_args, cos[:N_META], sin[:N_META], tm=N_META)
    _, k_m, v_m, _, _, _, _, glr_m, _, kmeta = meta
    q, k, v, sz, smz, sgg, sgm, glr, qm, km = _inproj(
        x, *proj_args, cos[N_META:], sin[N_META:], tm=256)

    ya = _gla(q, k, v, glr, sz, k_m, v_m, glr_m, gw, gb, gng, tc=512)
    yb = _mla(qm, km, kmeta, wuv, smz, tq=128, tk=256)
    return _outproj(ya, yb, sgg, sgm, x, gla_proj[0].astype(BF16), mla_proj[0].astype(BF16),
                    w_out[0].astype(BF16), fng, tm=512)
```

```python
import functools
import math

import jax
import jax.numpy as jnp
from jax import lax
from jax.experimental import pallas as pl
from jax.experimental.pallas import tpu as pltpu

F32 = jnp.float32
BF16 = jnp.bfloat16

D_MODEL = 1024
N_META = 16
EPS = 1e-6

GLA_HEADS = 4
GLA_DK = 128
GLA_DV = 256
GLA_GATE_RANK = 16
GLA_GATE_NORMALIZER = 16.0
GLA_CHUNK = 64
GLA_KW = GLA_HEADS * GLA_DK
GLA_VW = GLA_HEADS * GLA_DV

MLA_HEADS = 8
MLA_NOPE = 128
MLA_ROPE = 64
MLA_DV = 128
MLA_Q_RANK = 256
MLA_KV_RANK = 128
MLA_QK = MLA_NOPE + MLA_ROPE
MLA_VW = MLA_HEADS * MLA_DV
ROPE_BASE = 10000.0

SPLITS = (GLA_KW, GLA_KW, GLA_VW, GLA_GATE_RANK, GLA_VW,
          MLA_Q_RANK, MLA_KV_RANK, MLA_ROPE, MLA_VW, D_MODEL, D_MODEL)

LANES = 128
SUBLANES = 8
SOFTMAX_SLAB = 32
MLA_KPAD = 2 * LANES
W_MAIN = 2 * GLA_KW + 2 * GLA_VW + MLA_VW + 2 * D_MODEL
W_SMALL = MLA_Q_RANK + MLA_KV_RANK + 3 * LANES
VMEM_LIMIT = 56 * 1024 * 1024


def _rms(x, g):
    return x * lax.rsqrt(jnp.mean(x * x, axis=-1, keepdims=True) + EPS) * g


def _dot(a, b):
    return jnp.dot(a, b, preferred_element_type=F32)


def _dot_nt(a, b):
    return lax.dot_general(a, b, (((1,), (1,)), ((), ())), preferred_element_type=F32)


def _dot_tn(a, b):
    return lax.dot_general(a, b, (((0,), (0,)), ((), ())), preferred_element_type=F32)


def _inproj_kernel(x_ref, ng_ref, w1_ref, qg_ref, kvg_ref, wuq_ref, wukt_ref, cos_ref, sin_ref,
                   q_ref, k_ref, v_ref, sz_ref, smz_ref, sgg_ref, sgm_ref, glr_ref, qm_ref, km_ref):
    x = x_ref[0]
    u = _rms(x, ng_ref[...]).astype(BF16)

    def proj(lo, width):
        return _dot(u, w1_ref[:, lo:lo + width])

    o = 0
    q_ref[0] = (proj(o, GLA_KW) * (GLA_DK ** -0.5)).astype(BF16); o += GLA_KW
    k_ref[0] = proj(o, GLA_KW).astype(BF16); o += GLA_KW
    v_ref[0] = proj(o, GLA_VW).astype(BF16); o += GLA_VW
    sz_ref[0] = jax.nn.silu(proj(o, GLA_VW)).astype(BF16); o += GLA_VW
    smz_ref[0] = jax.nn.silu(proj(o, MLA_VW)).astype(BF16); o += MLA_VW
    sgg_ref[0] = jax.nn.sigmoid(proj(o, D_MODEL)).astype(BF16); o += D_MODEL
    sgm_ref[0] = jax.nn.sigmoid(proj(o, D_MODEL)).astype(BF16); o += D_MODEL

    cq = proj(o, MLA_Q_RANK); o += MLA_Q_RANK
    ckv = proj(o, MLA_KV_RANK); o += MLA_KV_RANK
    kr = proj(o, LANES); o += LANES
    krs = proj(o, LANES); o += LANES
    glr_ref[0] = proj(o, LANES).astype(BF16)

    cos = cos_ref[...]
    sin = sin_ref[...]
    km_ref[0, :, 0:LANES] = _rms(ckv, kvg_ref[...]).astype(BF16)
    km_ref[0, :, LANES:MLA_KPAD] = (kr * cos + krs * sin).astype(BF16)

    cqn = _rms(cq, qg_ref[...]).astype(BF16)
    scale = MLA_QK ** -0.5 * math.log2(math.e)
    nope_w = MLA_HEADS * MLA_NOPE
    rope_w = MLA_HEADS * MLA_ROPE
    for h in range(MLA_HEADS):
        qn = _dot(cqn, wuq_ref[:, h * MLA_NOPE:(h + 1) * MLA_NOPE]).astype(BF16)
        qm_ref[0, h, :, 0:LANES] = (_dot(qn, wukt_ref[h]) * scale).astype(BF16)
    lane = lax.broadcasted_iota(jnp.int32, (x.shape[0], LANES), 1)
    for g in range(MLA_HEADS // 2):
        a = _dot(cqn, wuq_ref[:, nope_w + g * LANES:nope_w + (g + 1) * LANES])
        b = _dot(cqn, wuq_ref[:, nope_w + rope_w + g * LANES:nope_w + rope_w + (g + 1) * LANES])
        r = (a * cos + b * sin) * scale
        qm_ref[0, 2 * g, :, LANES:MLA_KPAD] = jnp.where(lane < MLA_ROPE, r, 0.0).astype(BF16)
        qm_ref[0, 2 * g + 1, :, LANES:MLA_KPAD] = jnp.where(lane >= MLA_ROPE, r, 0.0).astype(BF16)


def _inproj(x, ng, w1, qg, kvg, wuq, wukt, cos, sin, tm):
    B, S, _ = x.shape
    grid = (B, S // tm)
    tok = lambda w: pl.BlockSpec((1, tm, w), lambda b, i: (b, i, 0))
    full = lambda a: pl.BlockSpec(a.shape, lambda b, i: (0,) * a.ndim)
    out_shapes = [
        jax.ShapeDtypeStruct((B, S, GLA_KW), BF16),
        jax.ShapeDtypeStruct((B, S, GLA_KW), BF16),
        jax.ShapeDtypeStruct((B, S, GLA_VW), BF16),
        jax.ShapeDtypeStruct((B, S, GLA_VW), BF16),
        jax.ShapeDtypeStruct((B, S, MLA_VW), BF16),
        jax.ShapeDtypeStruct((B, S, D_MODEL), BF16),
        jax.ShapeDtypeStruct((B, S, D_MODEL), BF16),
        jax.ShapeDtypeStruct((B, S, LANES), BF16),
        jax.ShapeDtypeStruct((B, MLA_HEADS, S, MLA_KPAD), BF16),
        jax.ShapeDtypeStruct((B, S, MLA_KPAD), BF16),
    ]
    out_specs = [tok(GLA_KW), tok(GLA_KW), tok(GLA_VW), tok(GLA_VW), tok(MLA_VW),
                 tok(D_MODEL), tok(D_MODEL), tok(LANES),
                 pl.BlockSpec((1, MLA_HEADS, tm, MLA_KPAD), lambda b, i: (b, 0, i, 0)),
                 tok(MLA_KPAD)]
    tab = pl.BlockSpec((tm, LANES), lambda b, i: (i, 0))
    return pl.pallas_call(
        _inproj_kernel,
        grid=grid,
        in_specs=[tok(D_MODEL), full(ng), full(w1), full(qg), full(kvg), full(wuq), full(wukt), tab, tab],
        out_specs=out_specs,
        out_shape=out_shapes,
        compiler_params=pltpu.CompilerParams(
            dimension_semantics=("parallel", "parallel"), vmem_limit_bytes=VMEM_LIMIT),
        name="inproj",
    )(x, ng, w1, qg, kvg, wuq, wukt, cos, sin)


def _tri(n):
    row = lax.broadcasted_iota(jnp.int32, (n, n), 0)
    col = lax.broadcasted_iota(jnp.int32, (n, n), 1)
    return col <= row


def _gla_log_gates(glr, gw_ref, gb_ref):
    return jax.nn.log_sigmoid(_dot(glr, gw_ref[...]) + gb_ref[...]) / GLA_GATE_NORMALIZER


def _cumsum_rows(tri, g):
    hi = g.astype(BF16)
    lo = (g - hi.astype(F32)).astype(BF16)
    return _dot(tri, hi) + _dot(tri, lo)


def _gla_kernel(q_ref, k_ref, v_ref, glr_ref, sz_ref, km_ref, vm_ref, glrm_ref,
                gw_ref, gb_ref, gng_ref, o_ref, st_ref, oi_ref, u_ref, *, n_chunks):
    C = GLA_CHUNK

    @pl.when(pl.program_id(1) == 0)
    def _init_state():
        b = _cumsum_rows(_tri(N_META).astype(BF16), _gla_log_gates(glrm_ref[0], gw_ref, gb_ref))
        kl = (km_ref[0].astype(F32) * jnp.exp(b[-1:, :] - b)).astype(BF16)
        for h in range(GLA_HEADS):
            st_ref[h] = _dot_tn(vm_ref[0, :, h * GLA_DV:(h + 1) * GLA_DV],
                                kl[:, h * GLA_DK:(h + 1) * GLA_DK])

    causal = _tri(C)
    tri = causal.astype(BF16)
    gng = gng_ref[...]
    heads = [(slice(h * GLA_DK, (h + 1) * GLA_DK), slice(h * GLA_DV, (h + 1) * GLA_DV))
             for h in range(GLA_HEADS)]
    chunks = [slice(c * C, (c + 1) * C) for c in range(n_chunks)]

    g = _gla_log_gates(glr_ref[0], gw_ref, gb_ref)
    b = jnp.concatenate([_cumsum_rows(tri, g[rows]) for rows in chunks], axis=0)
    b3 = b.reshape(n_chunks, C, GLA_KW)
    b_last = b3[:, C - 1:C, :]
    q = q_ref[0].astype(F32)
    k = k_ref[0].astype(F32)
    qe = (q * jnp.exp(b)).astype(BF16)
    ke = (k * jnp.exp(-b)).astype(BF16)
    kl = (k * jnp.exp(b_last - b3).reshape(n_chunks * C, GLA_KW)).astype(BF16)
    decay = jnp.exp(b_last)

    for c, rows in enumerate(chunks):
        for h, (ks, vs) in enumerate(heads):
            a = jnp.where(causal, _dot_nt(qe[rows, ks], ke[rows, ks]), 0.0).astype(BF16)
            oi_ref[rows, vs] = _dot(a, v_ref[0, rows, vs])
            u_ref[c, h] = _dot_tn(v_ref[0, rows, vs], kl[rows, ks])

    for c, rows in enumerate(chunks):
        for h, (ks, vs) in enumerate(heads):
            st = st_ref[h]
            o = oi_ref[rows, vs] + _dot_nt(qe[rows, ks], st.astype(BF16))
            st_ref[h] = st * decay[c, :, ks] + u_ref[c, h]
            o = _rms(o, gng) * sz_ref[0, rows, vs].astype(F32)
            o_ref[0, rows, vs] = o.astype(BF16)


def _gla(q, k, v, glr, sz, km, vm, glrm, gw, gb, gng, tc):
    B, S, _ = q.shape
    grid = (B, S // tc)
    tok = lambda w: pl.BlockSpec((1, tc, w), lambda b, i: (b, i, 0))
    full = lambda a: pl.BlockSpec(a.shape, lambda b, i: (0,) * a.ndim)
    return pl.pallas_call(
        functools.partial(_gla_kernel, n_chunks=tc // GLA_CHUNK),
        grid=grid,
        in_specs=[tok(GLA_KW), tok(GLA_KW), tok(GLA_VW), tok(LANES), tok(GLA_VW),
                  full(km), full(vm), full(glrm), full(gw), full(gb), full(gng)],
        out_specs=tok(GLA_VW),
        out_shape=jax.ShapeDtypeStruct((B, S, GLA_VW), BF16),
        scratch_shapes=[pltpu.VMEM((GLA_HEADS, GLA_DV, GLA_DK), F32),
                        pltpu.VMEM((tc, GLA_VW), F32),
                        pltpu.VMEM((tc // GLA_CHUNK, GLA_HEADS, GLA_DV, GLA_DK), F32)],
        compiler_params=pltpu.CompilerParams(
            dimension_semantics=("parallel", "arbitrary"), vmem_limit_bytes=VMEM_LIMIT),
        name="gla",
    )(q, k, v, glr, sz, km, vm, glrm, gw, gb, gng)


def _mla_kernel(qm_ref, km_ref, kmeta_ref, wuv_ref, smz_ref, o_ref,
                m_ref, l_ref, alpha_ref, acc_ref, s_ref, p_ref, *, tq, tk, max_full):
    i = pl.program_id(1)
    cols = MLA_HEADS * tq
    n_full = (i * tq) // tk

    gw = MLA_KPAD
    hpg = gw // tq
    groups = [(g, slice(g * gw, (g + 1) * gw)) for g in range(cols // gw)]

    def scores(kb, g):
        return _dot_nt(kb, qm_ref[0, g * hpg:(g + 1) * hpg].reshape(gw, MLA_KPAD))

    def keys(k0):
        return km_ref[0, k0:k0 + tk, :]

    def add_values(kb, p, cs):
        acc_ref[:, cs] = alpha_ref[:, cs] * acc_ref[:, cs] + _dot_tn(kb[:, 0:MLA_KV_RANK], p)

    def softmax(buf, n, cs):
        slab = min(n, SOFTMAX_SLAB)
        fold = lambda x, op: functools.reduce(op, [x[r:r + SUBLANES] for r in range(0, slab, SUBLANES)])
        part = None
        for r in range(0, n, slab):
            top = fold(s_ref[buf, r:r + slab, cs], jnp.maximum)
            part = top if part is None else jnp.maximum(part, top)
        m_old = m_ref[:, cs]
        m_new = jnp.maximum(m_old, jnp.max(part, axis=0, keepdims=True))
        alpha = jnp.exp2(m_old - m_new)
        part = None
        for r in range(0, n, slab):
            p = jnp.exp2(s_ref[buf, r:r + slab, cs] - m_new)
            p_ref[buf, r:r + slab, cs] = p.astype(BF16)
            tot = fold(p, jnp.add)
            part = tot if part is None else part + tot
        l_ref[:, cs] = alpha * l_ref[:, cs] + jnp.sum(part, axis=0, keepdims=True)
        m_ref[:, cs] = m_new
        alpha_ref[:, cs] = alpha

    def run(nf):
        diag0 = nf * tk
        block_start = lambda t: diag0 if t == 0 else (t - 1) * tk

        m_ref[...] = jnp.full(m_ref.shape, -jnp.inf, F32)
        l_ref[...] = jnp.zeros(l_ref.shape, F32)
        alpha_ref[...] = jnp.zeros(alpha_ref.shape, F32)
        acc_ref[...] = jnp.zeros(acc_ref.shape, F32)

        ktok = diag0 + lax.broadcasted_iota(jnp.int32, (tk, gw), 0)
        qtok = i * tq + (lax.broadcasted_iota(jnp.int32, (tk, gw), 1) & (tq - 1))
        for g, cs in groups:
            s_ref[0, :, cs] = jnp.where(ktok <= qtok, scores(keys(diag0), g), -jnp.inf)

        for t in range(nf + 1):
            cur, nxt = t % 2, 1 - t % 2
            for g, cs in groups:
                if t > 0:
                    add_values(keys(block_start(t - 1)), p_ref[nxt, :, cs], cs)
                if t < nf:
                    s_ref[nxt, :, cs] = scores(keys(block_start(t + 1)), g)
                else:
                    s_ref[nxt, 0:N_META, cs] = scores(kmeta_ref[0], g)
                softmax(cur, tk, cs)
        last, meta = nf % 2, 1 - nf % 2
        for g, cs in groups:
            add_values(keys(block_start(nf)), p_ref[last, :, cs], cs)
            softmax(meta, N_META, cs)
            add_values(kmeta_ref[0], p_ref[meta, 0:N_META, cs], cs)

        o_lat = (acc_ref[...] / l_ref[...]).astype(BF16)
        for h in range(MLA_HEADS):
            vs = slice(h * MLA_DV, (h + 1) * MLA_DV)
            o = _dot_tn(o_lat[:, h * tq:(h + 1) * tq], wuv_ref[h])
            o_ref[0, :, vs] = (o * smz_ref[0, :, vs].astype(F32)).astype(BF16)

    for nf in range(max_full + 1):
        pl.when(n_full == nf)(functools.partial(run, nf))


def _mla(qm, km, kmeta, wuv, smz, tq, tk):
    B, H, S, _ = qm.shape
    assert tq & (tq - 1) == 0 and S % tk == 0 and tk % tq == 0
    grid = (B, S // tq)
    full = lambda a: pl.BlockSpec(a.shape, lambda b, i: (0,) * a.ndim)
    rows = H * tq
    return pl.pallas_call(
        functools.partial(_mla_kernel, tq=tq, tk=tk, max_full=(S - tq) // tk),
        grid=grid,
        in_specs=[pl.BlockSpec((1, H, tq, MLA_KPAD), lambda b, i: (b, 0, i, 0)),
                  pl.BlockSpec((1, S, MLA_KPAD), lambda b, i: (b, 0, 0)),
                  full(kmeta), full(wuv),
                  pl.BlockSpec((1, tq, MLA_VW), lambda b, i: (b, i, 0))],
        out_specs=pl.BlockSpec((1, tq, MLA_VW), lambda b, i: (b, i, 0)),
        out_shape=jax.ShapeDtypeStruct((B, S, MLA_VW), BF16),
        scratch_shapes=[pltpu.VMEM((1, rows), F32), pltpu.VMEM((1, rows), F32),
                        pltpu.VMEM((1, rows), F32),
                        pltpu.VMEM((MLA_KV_RANK, rows), F32),
                        pltpu.VMEM((2, tk, rows), F32), pltpu.VMEM((2, tk, rows), BF16)],
        compiler_params=pltpu.CompilerParams(
            dimension_semantics=("parallel", "arbitrary"), vmem_limit_bytes=VMEM_LIMIT),
        name="mla",
    )(qm, km, kmeta, wuv, smz)


def _outproj_kernel(ya_ref, yb_ref, sgg_ref, sgm_ref, x_ref, gp_ref, mp_ref, wo_ref, fng_ref, o_ref):
    y_a = _dot(ya_ref[0], gp_ref[...])
    y_b = _dot(yb_ref[0], mp_ref[...])
    merged = sgg_ref[0].astype(F32) * y_a + sgm_ref[0].astype(F32) * y_b
    h = x_ref[0] + _dot(merged.astype(BF16), wo_ref[...])
    o_ref[0] = _rms(h, fng_ref[...])


def _outproj(ya, yb, sgg, sgm, x, gp, mp, wo, fng, tm):
    B, S, _ = x.shape
    grid = (B, S // tm)
    tok = pl.BlockSpec((1, tm, D_MODEL), lambda b, i: (b, i, 0))
    full = lambda a: pl.BlockSpec(a.shape, lambda b, i: (0,) * a.ndim)
    return pl.pallas_call(
        _outproj_kernel,
        grid=grid,
        in_specs=[tok, tok, tok, tok, tok, full(gp), full(mp), full(wo), full(fng)],
        out_specs=tok,
        out_shape=jax.ShapeDtypeStruct((B, S, D_MODEL), F32),
        compiler_params=pltpu.CompilerParams(
            dimension_semantics=("parallel", "parallel"), vmem_limit_bytes=VMEM_LIMIT),
        name="outproj",
    )(ya, yb, sgg, sgm, x, gp, mp, wo, fng)


def _swap_halves(w):
    half = w.shape[-1] // 2
    return jnp.concatenate([w[..., half:], w[..., :half]], axis=-1)


def _prep_weights(w_in, gla_gate_w, mla_w_uq, mla_w_ukv):
    cuts = [0]
    for s in SPLITS:
        cuts.append(cuts[-1] + s)
    (g_q, g_k, g_v, g_lr, g_z, m_cq, m_ckv, m_kr, m_z, gate_gla, gate_mla) = [
        w_in[:, cuts[n]:cuts[n + 1]] for n in range(len(SPLITS))]
    m_krs = _swap_halves(m_kr)
    pad = jnp.zeros((D_MODEL, LANES - GLA_GATE_RANK), w_in.dtype)
    w1 = jnp.concatenate([g_q, g_k, g_v, g_z, m_z, gate_gla, gate_mla,
                          m_cq, m_ckv, m_kr, m_kr, m_krs, m_krs, g_lr, pad], axis=1).astype(BF16)
    assert w1.shape == (D_MODEL, W_MAIN + W_SMALL)

    uq = mla_w_uq.reshape(MLA_Q_RANK, MLA_HEADS, MLA_QK)
    uq_nope = uq[:, :, :MLA_NOPE].reshape(MLA_Q_RANK, MLA_HEADS * MLA_NOPE)
    uq_rope = uq[:, :, MLA_NOPE:]
    wuq = jnp.concatenate([uq_nope, uq_rope.reshape(MLA_Q_RANK, -1),
                           _swap_halves(uq_rope).reshape(MLA_Q_RANK, -1)], axis=1).astype(BF16)

    ukv = mla_w_ukv.reshape(MLA_KV_RANK, MLA_HEADS, MLA_NOPE + MLA_DV)
    wukt = jnp.transpose(ukv[:, :, :MLA_NOPE], (1, 2, 0)).astype(BF16)
    wuv = jnp.transpose(ukv[:, :, MLA_NOPE:], (1, 0, 2)).astype(BF16)

    gw = jnp.concatenate([gla_gate_w, jnp.zeros((LANES - GLA_GATE_RANK, GLA_KW), gla_gate_w.dtype)],
                         axis=0).astype(BF16)
    return w1, wuq, wukt, wuv, gw


def _rope_tables(n):
    inv = 1.0 / (ROPE_BASE ** (jnp.arange(0, MLA_ROPE, 2, dtype=F32) / MLA_ROPE))
    ang = jnp.arange(n, dtype=F32)[:, None] * inv[None, :]
    cos, sin = jnp.cos(ang), jnp.sin(ang)
    cos = jnp.concatenate([cos, cos, cos, cos], axis=1)
    sin = jnp.concatenate([-sin, sin, -sin, sin], axis=1)
    return cos, sin


def kernel(x, meta_tokens, norm_g, w_in, gla_gate_w, gla_gate_b, gla_norm_g, gla_proj,
           mla_q_norm_g, mla_w_uq, mla_kv_norm_g, mla_w_ukv, mla_proj, w_out, final_norm_g):
    B, S, D = x.shape
    assert D == D_MODEL and norm_g.shape[0] == 1 and meta_tokens.shape == (N_META, D_MODEL)

    w1, wuq, wukt, wuv, gw = _prep_weights(w_in[0], gla_gate_w[0], mla_w_uq[0], mla_w_ukv[0])
    row = lambda a: a.reshape(1, -1).astype(F32)
    ng, qg, kvg = row(norm_g[0]), row(mla_q_norm_g[0]), row(mla_kv_norm_g[0])
    gb, gng, fng = row(gla_gate_b[0]), row(gla_norm_g[0]), row(final_norm_g)
    cos, sin = _rope_tables(N_META + S)

    proj_args = (ng, w1, qg, kvg, wuq, wukt)
    meta = _inproj(meta_tokens[None].astype(F32), *proj_args, cos[:N_META], sin[:N_META], tm=N_META)
    _, k_m, v_m, _, _, _, _, glr_m, _, kmeta = meta
    q, k, v, sz, smz, sgg, sgm, glr, qm, km = _inproj(
        x, *proj_args, cos[N_META:], sin[N_META:], tm=256)

    ya = _gla(q, k, v, glr, sz, k_m, v_m, glr_m, gw, gb, gng, tc=512)
    yb = _mla(qm, km, kmeta, wuv, smz, tq=128, tk=256)
    return _outproj(ya, yb, sgg, sgm, x, gla_proj[0].astype(BF16), mla_proj[0].astype(BF16),
                    w_out[0].astype(BF16), fng, tm=512)
```

```python
import functools
import math

import jax
import jax.numpy as jnp
from jax import lax
from jax.experimental import pallas as pl
from jax.experimental.pallas import tpu as pltpu

F32 = jnp.float32
BF16 = jnp.bfloat16

D_MODEL = 1024
N_META = 16
EPS = 1e-6

GLA_HEADS = 4
GLA_DK = 128
GLA_DV = 256
GLA_GATE_RANK = 16
GLA_GATE_NORMALIZER = 16.0
GLA_CHUNK = 64
GLA_KW = GLA_HEADS * GLA_DK
GLA_VW = GLA_HEADS * GLA_DV

MLA_HEADS = 8
MLA_NOPE = 128
MLA_ROPE = 64
MLA_DV = 128
MLA_Q_RANK = 256
MLA_KV_RANK = 128
MLA_QK = MLA_NOPE + MLA_ROPE
MLA_VW = MLA_HEADS * MLA_DV
ROPE_BASE = 10000.0

SPLITS = (GLA_KW, GLA_KW, GLA_VW, GLA_GATE_RANK, GLA_VW,
          MLA_Q_RANK, MLA_KV_RANK, MLA_ROPE, MLA_VW, D_MODEL, D_MODEL)

LANES = 128
SUBLANES = 8
SOFTMAX_SLAB = 32
MLA_KPAD = 2 * LANES
W_MAIN = 2 * GLA_KW + 2 * GLA_VW + MLA_VW + 2 * D_MODEL
W_SMALL = MLA_Q_RANK + MLA_KV_RANK + 3 * LANES
VMEM_LIMIT = 56 * 1024 * 1024


def _rms(x, g):
    return x * lax.rsqrt(jnp.mean(x * x, axis=-1, keepdims=True) + EPS) * g


def _dot(a, b):
    return jnp.dot(a, b, preferred_element_type=F32)


def _dot_nt(a, b):
    return lax.dot_general(a, b, (((1,), (1,)), ((), ())), preferred_element_type=F32)


def _dot_tn(a, b):
    return lax.dot_general(a, b, (((0,), (0,)), ((), ())), preferred_element_type=F32)


def _inproj_kernel(x_ref, ng_ref, w1_ref, qg_ref, kvg_ref, wuq_ref, wukt_ref, cos_ref, sin_ref,
                   q_ref, k_ref, v_ref, sz_ref, smz_ref, sgg_ref, sgm_ref, glr_ref, qm_ref, km_ref):
    x = x_ref[0]
    u = _rms(x, ng_ref[...]).astype(BF16)

    def proj(lo, width):
        return _dot(u, w1_ref[:, lo:lo + width])

    small = proj(W_MAIN, W_SMALL)
    cq = small[:, 0:MLA_Q_RANK]
    ckv = small[:, MLA_Q_RANK:MLA_Q_RANK + MLA_KV_RANK]
    o = MLA_Q_RANK + MLA_KV_RANK
    kr = small[:, o:o + LANES]
    krs = small[:, o + LANES:o + 2 * LANES]
    glr_ref[0] = small[:, o + 2 * LANES:o + 3 * LANES].astype(BF16)

    cos = cos_ref[...]
    sin = sin_ref[...]
    km_ref[0, :, 0:LANES] = _rms(ckv, kvg_ref[...]).astype(BF16)
    km_ref[0, :, LANES:MLA_KPAD] = (kr * cos + krs * sin).astype(BF16)
    cqn = _rms(cq, qg_ref[...]).astype(BF16)

    o = 0
    q_ref[0] = (proj(o, GLA_KW) * (GLA_DK ** -0.5)).astype(BF16); o += GLA_KW
    k_ref[0] = proj(o, GLA_KW).astype(BF16); o += GLA_KW

    qall = _dot(cqn, wuq_ref[...])
    v_ref[0] = proj(o, GLA_VW).astype(BF16); o += GLA_VW

    scale = MLA_QK ** -0.5 * math.log2(math.e)
    nope_w = MLA_HEADS * MLA_NOPE
    rope_w = MLA_HEADS * MLA_ROPE
    lane = lax.broadcasted_iota(jnp.int32, (x.shape[0], LANES), 1)
    for g in range(MLA_HEADS // 2):
        qn = qall[:, 2 * g * MLA_NOPE:(2 * g + 2) * MLA_NOPE].astype(BF16)
        qa = _dot(qn, wukt_ref[g]) * scale
        qm_ref[0, 2 * g, :, 0:LANES] = qa[:, 0:LANES].astype(BF16)
        qm_ref[0, 2 * g + 1, :, 0:LANES] = qa[:, LANES:2 * LANES].astype(BF16)
        a = qall[:, nope_w + g * LANES:nope_w + (g + 1) * LANES]
        b = qall[:, nope_w + rope_w + g * LANES:nope_w + rope_w + (g + 1) * LANES]
        r = (a * cos + b * sin) * scale
        qm_ref[0, 2 * g, :, LANES:MLA_KPAD] = jnp.where(lane < MLA_ROPE, r, 0.0).astype(BF16)
        qm_ref[0, 2 * g + 1, :, LANES:MLA_KPAD] = jnp.where(lane >= MLA_ROPE, r, 0.0).astype(BF16)

    sz_ref[0] = jax.nn.silu(proj(o, GLA_VW)).astype(BF16); o += GLA_VW
    smz_ref[0] = jax.nn.silu(proj(o, MLA_VW)).astype(BF16); o += MLA_VW
    sgg_ref[0] = jax.nn.sigmoid(proj(o, D_MODEL)).astype(BF16); o += D_MODEL
    sgm_ref[0] = jax.nn.sigmoid(proj(o, D_MODEL)).astype(BF16); o += D_MODEL


def _inproj(x, ng, w1, qg, kvg, wuq, wukt, cos, sin, tm):
    B, S, _ = x.shape
    grid = (B, S // tm)
    tok = lambda w: pl.BlockSpec((1, tm, w), lambda b, i: (b, i, 0))
    full = lambda a: pl.BlockSpec(a.shape, lambda b, i: (0,) * a.ndim)
    out_shapes = [
        jax.ShapeDtypeStruct((B, S, GLA_KW), BF16),
        jax.ShapeDtypeStruct((B, S, GLA_KW), BF16),
        jax.ShapeDtypeStruct((B, S, GLA_VW), BF16),
        jax.ShapeDtypeStruct((B, S, GLA_VW), BF16),
        jax.ShapeDtypeStruct((B, S, MLA_VW), BF16),
        jax.ShapeDtypeStruct((B, S, D_MODEL), BF16),
        jax.ShapeDtypeStruct((B, S, D_MODEL), BF16),
        jax.ShapeDtypeStruct((B, S, LANES), BF16),
        jax.ShapeDtypeStruct((B, MLA_HEADS, S, MLA_KPAD), BF16),
        jax.ShapeDtypeStruct((B, S, MLA_KPAD), BF16),
    ]
    out_specs = [tok(GLA_KW), tok(GLA_KW), tok(GLA_VW), tok(GLA_VW), tok(MLA_VW),
                 tok(D_MODEL), tok(D_MODEL), tok(LANES),
                 pl.BlockSpec((1, MLA_HEADS, tm, MLA_KPAD), lambda b, i: (b, 0, i, 0)),
                 tok(MLA_KPAD)]
    tab = pl.BlockSpec((tm, LANES), lambda b, i: (i, 0))
    return pl.pallas_call(
        _inproj_kernel,
        grid=grid,
        in_specs=[tok(D_MODEL), full(ng), full(w1), full(qg), full(kvg), full(wuq), full(wukt), tab, tab],
        out_specs=out_specs,
        out_shape=out_shapes,
        compiler_params=pltpu.CompilerParams(
            dimension_semantics=("parallel", "parallel"), vmem_limit_bytes=VMEM_LIMIT),
        name="inproj",
    )(x, ng, w1, qg, kvg, wuq, wukt, cos, sin)


def _tri(n):
    row = lax.broadcasted_iota(jnp.int32, (n, n), 0)
    col = lax.broadcasted_iota(jnp.int32, (n, n), 1)
    return col <= row


def _gla_log_gates(glr, gw_ref, gb_ref):
    return jax.nn.log_sigmoid(_dot(glr, gw_ref[...]) + gb_ref[...]) / GLA_GATE_NORMALIZER


def _cumsum_rows(tri, g):
    hi = g.astype(BF16)
    lo = (g - hi.astype(F32)).astype(BF16)
    return _dot(tri, hi) + _dot(tri, lo)


def _gla_kernel(q_ref, k_ref, v_ref, glr_ref, sz_ref, km_ref, vm_ref, glrm_ref,
                gw_ref, gb_ref, gng_ref, o_ref, st_ref, oi_ref, u_ref, *, n_chunks):
    C = GLA_CHUNK

    @pl.when(pl.program_id(1) == 0)
    def _init_state():
        b = _cumsum_rows(_tri(N_META).astype(BF16), _gla_log_gates(glrm_ref[0], gw_ref, gb_ref))
        kl = (km_ref[0].astype(F32) * jnp.exp(b[-1:, :] - b)).astype(BF16)
        for h in range(GLA_HEADS):
            st_ref[h] = _dot_tn(vm_ref[0, :, h * GLA_DV:(h + 1) * GLA_DV],
                                kl[:, h * GLA_DK:(h + 1) * GLA_DK])

    causal = _tri(C)
    tri = causal.astype(BF16)
    gng = gng_ref[...]
    heads = [(slice(h * GLA_DK, (h + 1) * GLA_DK), slice(h * GLA_DV, (h + 1) * GLA_DV))
             for h in range(GLA_HEADS)]
    chunks = [slice(c * C, (c + 1) * C) for c in range(n_chunks)]

    g = _gla_log_gates(glr_ref[0], gw_ref, gb_ref)
    b = jnp.concatenate([_cumsum_rows(tri, g[rows]) for rows in chunks], axis=0)
    b3 = b.reshape(n_chunks, C, GLA_KW)
    b_last = b3[:, C - 1:C, :]
    q = q_ref[0].astype(F32)
    k = k_ref[0].astype(F32)
    qe = (q * jnp.exp(b)).astype(BF16)
    ke = (k * jnp.exp(-b)).astype(BF16)
    kl = (k * jnp.exp(b_last - b3).reshape(n_chunks * C, GLA_KW)).astype(BF16)
    decay = jnp.exp(b_last)

    blocks = [(c, rows, h, ks, vs) for c, rows in enumerate(chunks) for h, (ks, vs) in enumerate(heads)]
    a = [jnp.where(causal, _dot_nt(qe[rows, ks], ke[rows, ks]), 0.0).astype(BF16)
         for _, rows, _, ks, _ in blocks]
    for c, rows, h, ks, vs in blocks:
        u_ref[c, h] = _dot_tn(v_ref[0, rows, vs], kl[rows, ks])
    for a_ch, (_, rows, _, _, vs) in zip(a, blocks):
        oi_ref[rows, vs] = _dot(a_ch, v_ref[0, rows, vs])

    for c, rows in enumerate(chunks):
        for h, (ks, vs) in enumerate(heads):
            st = st_ref[h]
            o = oi_ref[rows, vs] + _dot_nt(qe[rows, ks], st.astype(BF16))
            st_ref[h] = st * decay[c, :, ks] + u_ref[c, h]
            o = _rms(o, gng) * sz_ref[0, rows, vs].astype(F32)
            o_ref[0, rows, vs] = o.astype(BF16)


def _gla(q, k, v, glr, sz, km, vm, glrm, gw, gb, gng, tc):
    B, S, _ = q.shape
    grid = (B, S // tc)
    tok = lambda w: pl.BlockSpec((1, tc, w), lambda b, i: (b, i, 0))
    full = lambda a: pl.BlockSpec(a.shape, lambda b, i: (0,) * a.ndim)
    return pl.pallas_call(
        functools.partial(_gla_kernel, n_chunks=tc // GLA_CHUNK),
        grid=grid,
        in_specs=[tok(GLA_KW), tok(GLA_KW), tok(GLA_VW), tok(LANES), tok(GLA_VW),
                  full(km), full(vm), full(glrm), full(gw), full(gb), full(gng)],
        out_specs=tok(GLA_VW),
        out_shape=jax.ShapeDtypeStruct((B, S, GLA_VW), BF16),
        scratch_shapes=[pltpu.VMEM((GLA_HEADS, GLA_DV, GLA_DK), F32),
                        pltpu.VMEM((tc, GLA_VW), F32),
                        pltpu.VMEM((tc // GLA_CHUNK, GLA_HEADS, GLA_DV, GLA_DK), F32)],
        compiler_params=pltpu.CompilerParams(
            dimension_semantics=("parallel", "arbitrary"), vmem_limit_bytes=VMEM_LIMIT),
        name="gla",
    )(q, k, v, glr, sz, km, vm, glrm, gw, gb, gng)


def _mla_kernel(qm_ref, km_ref, kmeta_ref, wuv_ref, smz_ref, o_ref,
                m_ref, l_ref, alpha_ref, acc_ref, s_ref, p_ref, *, tq, tk, max_full):
    i = pl.program_id(1)
    cols = MLA_HEADS * tq
    n_full = (i * tq) // tk

    gw = MLA_KPAD
    hpg = gw // tq
    groups = [(g, slice(g * gw, (g + 1) * gw)) for g in range(cols // gw)]

    def scores(kb, g):
        return _dot_nt(kb, qm_ref[0, g * hpg:(g + 1) * hpg].reshape(gw, MLA_KPAD))

    def keys(k0):
        return km_ref[0, k0:k0 + tk, :]

    def add_values(kb, p, cs):
        acc_ref[:, cs] = alpha_ref[:, cs] * acc_ref[:, cs] + _dot_tn(kb[:, 0:MLA_KV_RANK], p)

    def softmax(buf, n, cs):
        slab = min(n, SOFTMAX_SLAB)
        fold = lambda x, op: functools.reduce(op, [x[r:r + SUBLANES] for r in range(0, slab, SUBLANES)])
        part = None
        for r in range(0, n, slab):
            top = fold(s_ref[buf, r:r + slab, cs], jnp.maximum)
            part = top if part is None else jnp.maximum(part, top)
        m_old = m_ref[:, cs]
        m_new = jnp.maximum(m_old, jnp.max(part, axis=0, keepdims=True))
        alpha = jnp.exp2(m_old - m_new)
        part = None
        for r in range(0, n, slab):
            p = jnp.exp2(s_ref[buf, r:r + slab, cs] - m_new)
            p_ref[buf, r:r + slab, cs] = p.astype(BF16)
            tot = fold(p, jnp.add)
            part = tot if part is None else part + tot
        l_ref[:, cs] = alpha * l_ref[:, cs] + jnp.sum(part, axis=0, keepdims=True)
        m_ref[:, cs] = m_new
        alpha_ref[:, cs] = alpha

    def run(nf):
        diag0 = nf * tk
        block_start = lambda t: diag0 if t == 0 else (t - 1) * tk

        m_ref[...] = jnp.full(m_ref.shape, -jnp.inf, F32)
        l_ref[...] = jnp.zeros(l_ref.shape, F32)
        alpha_ref[...] = jnp.zeros(alpha_ref.shape, F32)
        acc_ref[...] = jnp.zeros(acc_ref.shape, F32)

        ktok = diag0 + lax.broadcasted_iota(jnp.int32, (tk, gw), 0)
        qtok = i * tq + (lax.broadcasted_iota(jnp.int32, (tk, gw), 1) & (tq - 1))
        for g, cs in groups:
            s_ref[0, :, cs] = jnp.where(ktok <= qtok, scores(keys(diag0), g), -jnp.inf)

        for t in range(nf + 1):
            cur, nxt = t % 2, 1 - t % 2
            for g, cs in groups:
                if t > 0:
                    add_values(keys(block_start(t - 1)), p_ref[nxt, :, cs], cs)
                if t < nf:
                    s_ref[nxt, :, cs] = scores(keys(block_start(t + 1)), g)
                else:
                    s_ref[nxt, 0:N_META, cs] = scores(kmeta_ref[0], g)
                softmax(cur, tk, cs)
        last, meta = nf % 2, 1 - nf % 2
        for g, cs in groups:
            add_values(keys(block_start(nf)), p_ref[last, :, cs], cs)
            softmax(meta, N_META, cs)
            add_values(kmeta_ref[0], p_ref[meta, 0:N_META, cs], cs)

        o_lat = (acc_ref[...] / l_ref[...]).astype(BF16)
        for h in range(MLA_HEADS):
            vs = slice(h * MLA_DV, (h + 1) * MLA_DV)
            o = _dot_tn(o_lat[:, h * tq:(h + 1) * tq], wuv_ref[h])
            o_ref[0, :, vs] = (o * smz_ref[0, :, vs].astype(F32)).astype(BF16)

    for nf in range(max_full + 1):
        pl.when(n_full == nf)(functools.partial(run, nf))


def _mla(qm, km, kmeta, wuv, smz, tq, tk):
    B, H, S, _ = qm.shape
    assert tq & (tq - 1) == 0 and S % tk == 0 and tk % tq == 0
    grid = (B, S // tq)
    full = lambda a: pl.BlockSpec(a.shape, lambda b, i: (0,) * a.ndim)
    rows = H * tq
    return pl.pallas_call(
        functools.partial(_mla_kernel, tq=tq, tk=tk, max_full=(S - tq) // tk),
        grid=grid,
        in_specs=[pl.BlockSpec((1, H, tq, MLA_KPAD), lambda b, i: (b, 0, i, 0)),
                  pl.BlockSpec((1, S, MLA_KPAD), lambda b, i: (b, 0, 0)),
                  full(kmeta), full(wuv),
                  pl.BlockSpec((1, tq, MLA_VW), lambda b, i: (b, i, 0))],
        out_specs=pl.BlockSpec((1, tq, MLA_VW), lambda b, i: (b, i, 0)),
        out_shape=jax.ShapeDtypeStruct((B, S, MLA_VW), BF16),
        scratch_shapes=[pltpu.VMEM((1, rows), F32), pltpu.VMEM((1, rows), F32),
                        pltpu.VMEM((1, rows), F32),
                        pltpu.VMEM((MLA_KV_RANK, rows), F32),
                        pltpu.VMEM((2, tk, rows), F32), pltpu.VMEM((2, tk, rows), BF16)],
        compiler_params=pltpu.CompilerParams(
            dimension_semantics=("parallel", "arbitrary"), vmem_limit_bytes=VMEM_LIMIT),
        name="mla",
    )(qm, km, kmeta, wuv, smz)


def _outproj_kernel(ya_ref, yb_ref, sgg_ref, sgm_ref, x_ref, gp_ref, mp_ref, wo_ref, fng_ref, o_ref):
    y_a = _dot(ya_ref[0], gp_ref[...])
    y_b = _dot(yb_ref[0], mp_ref[...])
    merged = sgg_ref[0].astype(F32) * y_a + sgm_ref[0].astype(F32) * y_b
    h = x_ref[0] + _dot(merged.astype(BF16), wo_ref[...])
    o_ref[0] = _rms(h, fng_ref[...])


def _outproj(ya, yb, sgg, sgm, x, gp, mp, wo, fng, tm):
    B, S, _ = x.shape
    grid = (B, S // tm)
    tok = pl.BlockSpec((1, tm, D_MODEL), lambda b, i: (b, i, 0))
    full = lambda a: pl.BlockSpec(a.shape, lambda b, i: (0,) * a.ndim)
    return pl.pallas_call(
        _outproj_kernel,
        grid=grid,
        in_specs=[tok, tok, tok, tok, tok, full(gp), full(mp), full(wo), full(fng)],
        out_specs=tok,
        out_shape=jax.ShapeDtypeStruct((B, S, D_MODEL), F32),
        compiler_params=pltpu.CompilerParams(
            dimension_semantics=("parallel", "parallel"), vmem_limit_bytes=VMEM_LIMIT),
        name="outproj",
    )(ya, yb, sgg, sgm, x, gp, mp, wo, fng)


def _swap_halves(w):
    half = w.shape[-1] // 2
    return jnp.concatenate([w[..., half:], w[..., :half]], axis=-1)


def _prep_weights(w_in, gla_gate_w, mla_w_uq, mla_w_ukv):
    cuts = [0]
    for s in SPLITS:
        cuts.append(cuts[-1] + s)
    (g_q, g_k, g_v, g_lr, g_z, m_cq, m_ckv, m_kr, m_z, gate_gla, gate_mla) = [
        w_in[:, cuts[n]:cuts[n + 1]] for n in range(len(SPLITS))]
    m_krs = _swap_halves(m_kr)
    pad = jnp.zeros((D_MODEL, LANES - GLA_GATE_RANK), w_in.dtype)
    w1 = jnp.concatenate([g_q, g_k, g_v, g_z, m_z, gate_gla, gate_mla,
                          m_cq, m_ckv, m_kr, m_kr, m_krs, m_krs, g_lr, pad], axis=1).astype(BF16)
    assert w1.shape == (D_MODEL, W_MAIN + W_SMALL)

    uq = mla_w_uq.reshape(MLA_Q_RANK, MLA_HEADS, MLA_QK)
    uq_nope = uq[:, :, :MLA_NOPE].reshape(MLA_Q_RANK, MLA_HEADS * MLA_NOPE)
    uq_rope = uq[:, :, MLA_NOPE:]
    wuq = jnp.concatenate([uq_nope, uq_rope.reshape(MLA_Q_RANK, -1),
                           _swap_halves(uq_rope).reshape(MLA_Q_RANK, -1)], axis=1).astype(BF16)

    ukv = mla_w_ukv.reshape(MLA_KV_RANK, MLA_HEADS, MLA_NOPE + MLA_DV)
    wukt = jnp.transpose(ukv[:, :, :MLA_NOPE], (1, 2, 0)).astype(BF16)
    zero = jnp.zeros_like(wukt[0::2])
    wukt = jnp.concatenate([jnp.concatenate([wukt[0::2], zero], axis=2),
                            jnp.concatenate([zero, wukt[1::2]], axis=2)], axis=1)
    wuv = jnp.transpose(ukv[:, :, MLA_NOPE:], (1, 0, 2)).astype(BF16)

    gw = jnp.concatenate([gla_gate_w, jnp.zeros((LANES - GLA_GATE_RANK, GLA_KW), gla_gate_w.dtype)],
                         axis=0).astype(BF16)
    return w1, wuq, wukt, wuv, gw


def _rope_tables(n):
    inv = 1.0 / (ROPE_BASE ** (jnp.arange(0, MLA_ROPE, 2, dtype=F32) / MLA_ROPE))
    ang = jnp.arange(n, dtype=F32)[:, None] * inv[None, :]
    cos, sin = jnp.cos(ang), jnp.sin(ang)
    cos = jnp.concatenate([cos, cos, cos, cos], axis=1)
    sin = jnp.concatenate([-sin, sin, -sin, sin], axis=1)
    return cos, sin


def kernel(x, meta_tokens, norm_g, w_in, gla_gate_w, gla_gate_b, gla_norm_g, gla_proj,
           mla_q_norm_g, mla_w_uq, mla_kv_norm_g, mla_w_ukv, mla_proj, w_out, final_norm_g):
    B, S, D = x.shape
    assert D == D_MODEL and norm_g.shape[0] == 1 and meta_tokens.shape == (N_META, D_MODEL)

    w1, wuq, wukt, wuv, gw = _prep_weights(w_in[0], gla_gate_w[0], mla_w_uq[0], mla_w_ukv[0])
    row = lambda a: a.reshape(1, -1).astype(F32)
    ng, qg, kvg = row(norm_g[0]), row(mla_q_norm_g[0]), row(mla_kv_norm_g[0])
    gb, gng, fng = row(gla_gate_b[0]), row(gla_norm_g[0]), row(final_norm_g)
    cos, sin = _rope_tables(N_META + S)

    proj_args = (ng, w1, qg, kvg, wuq, wukt)
    meta = _inproj(meta_tokens[None].astype(F32), *proj_args, cos[:N_META], sin[:N_META], tm=N_META)
    _, k_m, v_m, _, _, _, _, glr_m, _, kmeta = meta
    q, k, v, sz, smz, sgg, sgm, glr, qm, km = _inproj(
        x, *proj_args, cos[N_META:], sin[N_META:], tm=512)

    ya = _gla(q, k, v, glr, sz, k_m, v_m, glr_m, gw, gb, gng, tc=512)
    yb = _mla(qm, km, kmeta, wuv, smz, tq=128, tk=256)
    return _outproj(ya, yb, sgg, sgm, x, gla_proj[0].astype(BF16), mla_proj[0].astype(BF16),
                    w_out[0].astype(BF16), fng, tm=1024)
```

```python
import functools
import math

import jax
import jax.numpy as jnp
from jax import lax
from jax.experimental import pallas as pl
from jax.experimental.pallas import tpu as pltpu

F32 = jnp.float32
BF16 = jnp.bfloat16

D_MODEL = 1024
N_META = 16
EPS = 1e-6

GLA_HEADS = 4
GLA_DK = 128
GLA_DV = 256
GLA_GATE_RANK = 16
GLA_GATE_NORMALIZER = 16.0
GLA_CHUNK = 64
GLA_KW = GLA_HEADS * GLA_DK
GLA_VW = GLA_HEADS * GLA_DV

MLA_HEADS = 8
MLA_NOPE = 128
MLA_ROPE = 64
MLA_DV = 128
MLA_Q_RANK = 256
MLA_KV_RANK = 128
MLA_QK = MLA_NOPE + MLA_ROPE
MLA_VW = MLA_HEADS * MLA_DV
ROPE_BASE = 10000.0

SPLITS = (GLA_KW, GLA_KW, GLA_VW, GLA_GATE_RANK, GLA_VW,
          MLA_Q_RANK, MLA_KV_RANK, MLA_ROPE, MLA_VW, D_MODEL, D_MODEL)

LANES = 128
SUBLANES = 8
SOFTMAX_SLAB = 32
MLA_KPAD = 2 * LANES
MLA_VT_ROWS = MLA_KV_RANK + 16
W_MAIN = 2 * GLA_KW + 2 * GLA_VW + MLA_VW + 2 * D_MODEL
W_SMALL = MLA_Q_RANK + MLA_KV_RANK + 3 * LANES
VMEM_LIMIT = 56 * 1024 * 1024


def _rms(x, g):
    return x * lax.rsqrt(jnp.mean(x * x, axis=-1, keepdims=True) + EPS) * g


def _dot(a, b):
    return jnp.dot(a, b, preferred_element_type=F32)


def _dot_nt(a, b):
    return lax.dot_general(a, b, (((1,), (1,)), ((), ())), preferred_element_type=F32)


def _dot_tn(a, b):
    return lax.dot_general(a, b, (((0,), (0,)), ((), ())), preferred_element_type=F32)


def _inproj_kernel(x_ref, ng_ref, w1_ref, qg_ref, kvg_ref, wuq_ref, wukt_ref, cos_ref, sin_ref,
                   q_ref, k_ref, v_ref, sz_ref, smz_ref, sgg_ref, sgm_ref, glr_ref, qm_ref, km_ref,
                   *maybe_kvt_ref):
    x = x_ref[0]
    u = _rms(x, ng_ref[...]).astype(BF16)

    def proj(lo, width):
        return _dot(u, w1_ref[:, lo:lo + width])

    small = proj(W_MAIN, W_SMALL)
    cq = small[:, 0:MLA_Q_RANK]
    ckv = small[:, MLA_Q_RANK:MLA_Q_RANK + MLA_KV_RANK]
    o = MLA_Q_RANK + MLA_KV_RANK
    kr = small[:, o:o + LANES]
    krs = small[:, o + LANES:o + 2 * LANES]
    glr_ref[0] = small[:, o + 2 * LANES:o + 3 * LANES].astype(BF16)

    cos = cos_ref[...]
    sin = sin_ref[...]
    ckvn = _rms(ckv, kvg_ref[...])
    km_ref[0, :, 0:LANES] = ckvn.astype(BF16)
    for kvt_ref in maybe_kvt_ref:
        kvt_ref[0, 0:MLA_KV_RANK, :] = ckvn.T.astype(BF16)
        kvt_ref[0, MLA_KV_RANK:MLA_VT_ROWS, :] = jnp.ones((MLA_VT_ROWS - MLA_KV_RANK, x.shape[0]), BF16)
    km_ref[0, :, LANES:MLA_KPAD] = (kr * cos + krs * sin).astype(BF16)
    cqn = _rms(cq, qg_ref[...]).astype(BF16)

    o = 0
    q_ref[0] = (proj(o, GLA_KW) * (GLA_DK ** -0.5)).astype(BF16); o += GLA_KW
    k_ref[0] = proj(o, GLA_KW).astype(BF16); o += GLA_KW

    qall = _dot(cqn, wuq_ref[...])
    v_ref[0] = proj(o, GLA_VW).astype(BF16); o += GLA_VW

    scale = MLA_QK ** -0.5 * math.log2(math.e)
    nope_w = MLA_HEADS * MLA_NOPE
    rope_w = MLA_HEADS * MLA_ROPE
    lane = lax.broadcasted_iota(jnp.int32, (x.shape[0], LANES), 1)
    for g in range(MLA_HEADS // 2):
        qn = qall[:, 2 * g * MLA_NOPE:(2 * g + 2) * MLA_NOPE].astype(BF16)
        qa = _dot(qn, wukt_ref[g]) * scale
        qm_ref[0, 2 * g, :, 0:LANES] = qa[:, 0:LANES].astype(BF16)
        qm_ref[0, 2 * g + 1, :, 0:LANES] = qa[:, LANES:2 * LANES].astype(BF16)
        a = qall[:, nope_w + g * LANES:nope_w + (g + 1) * LANES]
        b = qall[:, nope_w + rope_w + g * LANES:nope_w + rope_w + (g + 1) * LANES]
        r = (a * cos + b * sin) * scale
        qm_ref[0, 2 * g, :, LANES:MLA_KPAD] = jnp.where(lane < MLA_ROPE, r, 0.0).astype(BF16)
        qm_ref[0, 2 * g + 1, :, LANES:MLA_KPAD] = jnp.where(lane >= MLA_ROPE, r, 0.0).astype(BF16)

    sz_ref[0] = jax.nn.silu(proj(o, GLA_VW)).astype(BF16); o += GLA_VW
    smz_ref[0] = jax.nn.silu(proj(o, MLA_VW)).astype(BF16); o += MLA_VW
    sgg_ref[0] = jax.nn.sigmoid(proj(o, D_MODEL)).astype(BF16); o += D_MODEL
    sgm_ref[0] = jax.nn.sigmoid(proj(o, D_MODEL)).astype(BF16); o += D_MODEL


def _inproj(x, ng, w1, qg, kvg, wuq, wukt, cos, sin, tm, emit_vt):
    B, S, _ = x.shape
    grid = (B, S // tm)
    tok = lambda w: pl.BlockSpec((1, tm, w), lambda b, i: (b, i, 0))
    full = lambda a: pl.BlockSpec(a.shape, lambda b, i: (0,) * a.ndim)
    out_shapes = [
        jax.ShapeDtypeStruct((B, S, GLA_KW), BF16),
        jax.ShapeDtypeStruct((B, S, GLA_KW), BF16),
        jax.ShapeDtypeStruct((B, S, GLA_VW), BF16),
        jax.ShapeDtypeStruct((B, S, GLA_VW), BF16),
        jax.ShapeDtypeStruct((B, S, MLA_VW), BF16),
        jax.ShapeDtypeStruct((B, S, D_MODEL), BF16),
        jax.ShapeDtypeStruct((B, S, D_MODEL), BF16),
        jax.ShapeDtypeStruct((B, S, LANES), BF16),
        jax.ShapeDtypeStruct((B, MLA_HEADS, S, MLA_KPAD), BF16),
        jax.ShapeDtypeStruct((B, S, MLA_KPAD), BF16),
    ]
    out_specs = [tok(GLA_KW), tok(GLA_KW), tok(GLA_VW), tok(GLA_VW), tok(MLA_VW),
                 tok(D_MODEL), tok(D_MODEL), tok(LANES),
                 pl.BlockSpec((1, MLA_HEADS, tm, MLA_KPAD), lambda b, i: (b, 0, i, 0)),
                 tok(MLA_KPAD)]
    if emit_vt:
        out_shapes.append(jax.ShapeDtypeStruct((B, MLA_VT_ROWS, S), BF16))
        out_specs.append(pl.BlockSpec((1, MLA_VT_ROWS, tm), lambda b, i: (b, 0, i)))
    tab = pl.BlockSpec((tm, LANES), lambda b, i: (i, 0))
    return pl.pallas_call(
        _inproj_kernel,
        grid=grid,
        in_specs=[tok(D_MODEL), full(ng), full(w1), full(qg), full(kvg), full(wuq), full(wukt), tab, tab],
        out_specs=out_specs,
        out_shape=out_shapes,
        compiler_params=pltpu.CompilerParams(
            dimension_semantics=("parallel", "parallel"), vmem_limit_bytes=VMEM_LIMIT),
        name="inproj",
    )(x, ng, w1, qg, kvg, wuq, wukt, cos, sin)


def _tri(n):
    row = lax.broadcasted_iota(jnp.int32, (n, n), 0)
    col = lax.broadcasted_iota(jnp.int32, (n, n), 1)
    return col <= row


def _gla_log_gates(glr, gw_ref, gb_ref):
    return jax.nn.log_sigmoid(_dot(glr, gw_ref[...]) + gb_ref[...]) / GLA_GATE_NORMALIZER


def _cumsum_rows(tri, g):
    hi = g.astype(BF16)
    lo = (g - hi.astype(F32)).astype(BF16)
    return _dot(tri, hi) + _dot(tri, lo)


def _gla_kernel(q_ref, k_ref, v_ref, glr_ref, sz_ref, km_ref, vm_ref, glrm_ref,
                gw_ref, gb_ref, gng_ref, o_ref, st_ref, oi_ref, u_ref, *, n_chunks):
    C = GLA_CHUNK

    @pl.when(pl.program_id(1) == 0)
    def _init_state():
        b = _cumsum_rows(_tri(N_META).astype(BF16), _gla_log_gates(glrm_ref[0], gw_ref, gb_ref))
        kl = (km_ref[0].astype(F32) * jnp.exp(b[-1:, :] - b)).astype(BF16)
        for h in range(GLA_HEADS):
            st_ref[h] = _dot_tn(vm_ref[0, :, h * GLA_DV:(h + 1) * GLA_DV],
                                kl[:, h * GLA_DK:(h + 1) * GLA_DK])

    causal = _tri(C)
    tri = causal.astype(BF16)
    gng = gng_ref[...]
    heads = [(slice(h * GLA_DK, (h + 1) * GLA_DK), slice(h * GLA_DV, (h + 1) * GLA_DV))
             for h in range(GLA_HEADS)]
    chunks = [slice(c * C, (c + 1) * C) for c in range(n_chunks)]

    g = _gla_log_gates(glr_ref[0], gw_ref, gb_ref)
    b = jnp.concatenate([_cumsum_rows(tri, g[rows]) for rows in chunks], axis=0)
    b3 = b.reshape(n_chunks, C, GLA_KW)
    b_last = b3[:, C - 1:C, :]
    q = q_ref[0].astype(F32)
    k = k_ref[0].astype(F32)
    qe = (q * jnp.exp(b)).astype(BF16)
    ke = (k * jnp.exp(-b)).astype(BF16)
    kl = (k * jnp.exp(b_last - b3).reshape(n_chunks * C, GLA_KW)).astype(BF16)
    decay = jnp.exp(b_last)

    blocks = [(c, rows, h, ks, vs) for c, rows in enumerate(chunks) for h, (ks, vs) in enumerate(heads)]
    a = [jnp.where(causal, _dot_nt(qe[rows, ks], ke[rows, ks]), 0.0).astype(BF16)
         for _, rows, _, ks, _ in blocks]
    for c, rows, h, ks, vs in blocks:
        u_ref[c, h] = _dot_tn(v_ref[0, rows, vs], kl[rows, ks])
    for a_ch, (_, rows, _, _, vs) in zip(a, blocks):
        oi_ref[rows, vs] = _dot(a_ch, v_ref[0, rows, vs])

    for c, rows in enumerate(chunks):
        for h, (ks, vs) in enumerate(heads):
            st = st_ref[h]
            o = oi_ref[rows, vs] + _dot_nt(qe[rows, ks], st.astype(BF16))
            st_ref[h] = st * decay[c, :, ks] + u_ref[c, h]
            o = _rms(o, gng) * sz_ref[0, rows, vs].astype(F32)
            o_ref[0, rows, vs] = o.astype(BF16)


def _gla(q, k, v, glr, sz, km, vm, glrm, gw, gb, gng, tc):
    B, S, _ = q.shape
    grid = (B, S // tc)
    tok = lambda w: pl.BlockSpec((1, tc, w), lambda b, i: (b, i, 0))
    full = lambda a: pl.BlockSpec(a.shape, lambda b, i: (0,) * a.ndim)
    return pl.pallas_call(
        functools.partial(_gla_kernel, n_chunks=tc // GLA_CHUNK),
        grid=grid,
        in_specs=[tok(GLA_KW), tok(GLA_KW), tok(GLA_VW), tok(LANES), tok(GLA_VW),
                  full(km), full(vm), full(glrm), full(gw), full(gb), full(gng)],
        out_specs=tok(GLA_VW),
        out_shape=jax.ShapeDtypeStruct((B, S, GLA_VW), BF16),
        scratch_shapes=[pltpu.VMEM((GLA_HEADS, GLA_DV, GLA_DK), F32),
                        pltpu.VMEM((tc, GLA_VW), F32),
                        pltpu.VMEM((tc // GLA_CHUNK, GLA_HEADS, GLA_DV, GLA_DK), F32)],
        compiler_params=pltpu.CompilerParams(
            dimension_semantics=("parallel", "arbitrary"), vmem_limit_bytes=VMEM_LIMIT),
        name="gla",
    )(q, k, v, glr, sz, km, vm, glrm, gw, gb, gng)


def _mla_kernel(qm_ref, km_ref, kvt_ref, kmeta_ref, kvtmeta_ref, wuv_ref, smz_ref, o_ref,
                m_ref, l_ref, alpha_ref, acc_ref, s_ref, p_ref, bmax_ref, *, tq, tk, max_full):
    i = pl.program_id(1)
    cols = MLA_HEADS * tq
    n_full = (i * tq) // tk

    gw = MLA_KPAD
    hpg = gw // tq
    groups = [(g, slice(g * gw, (g + 1) * gw)) for g in range(cols // gw)]

    def scores(kb, g):
        return _dot_nt(kb, qm_ref[0, g * hpg:(g + 1) * hpg].reshape(gw, MLA_KPAD))

    def keys(k0):
        return km_ref[0, k0:k0 + tk, :]

    def values(k0):
        return kvt_ref[0, :, k0:k0 + tk]

    def add_values(vt, p, cs):
        r = _dot(vt, p)
        alpha = alpha_ref[:, cs]
        acc_ref[:, cs] = alpha * acc_ref[:, cs] + r[0:MLA_KV_RANK]
        l_ref[:, cs] = alpha * l_ref[:, cs] + r[MLA_KV_RANK:MLA_KV_RANK + 1]

    def fold(x, op):
        return functools.reduce(op, [x[r:r + SUBLANES] for r in range(0, x.shape[0], SUBLANES)])

    def put_scores(buf, n, cs, s):
        s_ref[buf, 0:n, cs] = s
        bmax_ref[buf, :, cs] = fold(s, jnp.maximum)

    def softmax(buf, n, cs):
        slab = min(n, SOFTMAX_SLAB)
        m_old = m_ref[:, cs]
        m_new = jnp.maximum(m_old, jnp.max(bmax_ref[buf, :, cs], axis=0, keepdims=True))
        alpha = jnp.exp2(m_old - m_new)
        for r in range(0, n, slab):
            p_ref[buf, r:r + slab, cs] = jnp.exp2(s_ref[buf, r:r + slab, cs] - m_new).astype(BF16)
        m_ref[:, cs] = m_new
        alpha_ref[:, cs] = alpha

    def run(nf):
        diag0 = nf * tk
        block_start = lambda t: diag0 if t == 0 else (t - 1) * tk

        m_ref[...] = jnp.full(m_ref.shape, -jnp.inf, F32)
        l_ref[...] = jnp.zeros(l_ref.shape, F32)
        alpha_ref[...] = jnp.zeros(alpha_ref.shape, F32)
        acc_ref[...] = jnp.zeros(acc_ref.shape, F32)

        ktok = diag0 + lax.broadcasted_iota(jnp.int32, (tk, gw), 0)
        qtok = i * tq + (lax.broadcasted_iota(jnp.int32, (tk, gw), 1) & (tq - 1))
        for g, cs in groups:
            put_scores(0, tk, cs, jnp.where(ktok <= qtok, scores(keys(diag0), g), -jnp.inf))

        for t in range(nf + 1):
            cur, nxt = t % 2, 1 - t % 2
            for g, cs in groups:
                if t > 0:
                    add_values(values(block_start(t - 1)), p_ref[nxt, :, cs], cs)
                if t < nf:
                    put_scores(nxt, tk, cs, scores(keys(block_start(t + 1)), g))
                else:
                    put_scores(nxt, N_META, cs, scores(kmeta_ref[0], g))
                softmax(cur, tk, cs)
        last, meta = nf % 2, 1 - nf % 2
        for g, cs in groups:
            add_values(values(block_start(nf)), p_ref[last, :, cs], cs)
            softmax(meta, N_META, cs)
            add_values(kvtmeta_ref[0], p_ref[meta, 0:N_META, cs], cs)

        o_lat = (acc_ref[...] / l_ref[...]).astype(BF16)
        for h in range(MLA_HEADS):
            vs = slice(h * MLA_DV, (h + 1) * MLA_DV)
            o = _dot_tn(o_lat[:, h * tq:(h + 1) * tq], wuv_ref[h])
            o_ref[0, :, vs] = (o * smz_ref[0, :, vs].astype(F32)).astype(BF16)

    for nf in range(max_full + 1):
        pl.when(n_full == nf)(functools.partial(run, nf))


def _mla(qm, km, kvt, kmeta, kvtmeta, wuv, smz, tq, tk):
    B, H, S, _ = qm.shape
    assert tq & (tq - 1) == 0 and S % tk == 0 and tk % tq == 0
    grid = (B, S // tq)
    full = lambda a: pl.BlockSpec(a.shape, lambda b, i: (0,) * a.ndim)
    rows = H * tq
    return pl.pallas_call(
        functools.partial(_mla_kernel, tq=tq, tk=tk, max_full=(S - tq) // tk),
        grid=grid,
        in_specs=[pl.BlockSpec((1, H, tq, MLA_KPAD), lambda b, i: (b, 0, i, 0)),
                  pl.BlockSpec((1, S, MLA_KPAD), lambda b, i: (b, 0, 0)),
                  pl.BlockSpec((1, MLA_VT_ROWS, S), lambda b, i: (b, 0, 0)),
                  full(kmeta), full(kvtmeta), full(wuv),
                  pl.BlockSpec((1, tq, MLA_VW), lambda b, i: (b, i, 0))],
        out_specs=pl.BlockSpec((1, tq, MLA_VW), lambda b, i: (b, i, 0)),
        out_shape=jax.ShapeDtypeStruct((B, S, MLA_VW), BF16),
        scratch_shapes=[pltpu.VMEM((1, rows), F32), pltpu.VMEM((1, rows), F32),
                        pltpu.VMEM((1, rows), F32),
                        pltpu.VMEM((MLA_KV_RANK, rows), F32),
                        pltpu.VMEM((2, tk, rows), F32), pltpu.VMEM((2, tk, rows), BF16),
                        pltpu.VMEM((2, SUBLANES, rows), F32)],
        compiler_params=pltpu.CompilerParams(
            dimension_semantics=("parallel", "arbitrary"), vmem_limit_bytes=VMEM_LIMIT,
        ),
        name="mla",
    )(qm, km, kvt, kmeta, kvtmeta, wuv, smz)


def _outproj_kernel(ya_ref, yb_ref, sgg_ref, sgm_ref, x_ref, gp_ref, mp_ref, wo_ref, fng_ref, o_ref):
    y_a = _dot(ya_ref[0], gp_ref[...])
    y_b = _dot(yb_ref[0], mp_ref[...])
    merged = sgg_ref[0].astype(F32) * y_a + sgm_ref[0].astype(F32) * y_b
    h = x_ref[0] + _dot(merged.astype(BF16), wo_ref[...])
    o_ref[0] = _rms(h, fng_ref[...])


def _outproj(ya, yb, sgg, sgm, x, gp, mp, wo, fng, tm):
    B, S, _ = x.shape
    grid = (B, S // tm)
    tok = pl.BlockSpec((1, tm, D_MODEL), lambda b, i: (b, i, 0))
    full = lambda a: pl.BlockSpec(a.shape, lambda b, i: (0,) * a.ndim)
    return pl.pallas_call(
        _outproj_kernel,
        grid=grid,
        in_specs=[tok, tok, tok, tok, tok, full(gp), full(mp), full(wo), full(fng)],
        out_specs=tok,
        out_shape=jax.ShapeDtypeStruct((B, S, D_MODEL), F32),
        compiler_params=pltpu.CompilerParams(
            dimension_semantics=("parallel", "parallel"), vmem_limit_bytes=VMEM_LIMIT),
        name="outproj",
    )(ya, yb, sgg, sgm, x, gp, mp, wo, fng)


def _swap_halves(w):
    half = w.shape[-1] // 2
    return jnp.concatenate([w[..., half:], w[..., :half]], axis=-1)


def _prep_weights(w_in, gla_gate_w, mla_w_uq, mla_w_ukv):
    cuts = [0]
    for s in SPLITS:
        cuts.append(cuts[-1] + s)
    (g_q, g_k, g_v, g_lr, g_z, m_cq, m_ckv, m_kr, m_z, gate_gla, gate_mla) = [
        w_in[:, cuts[n]:cuts[n + 1]] for n in range(len(SPLITS))]
    m_krs = _swap_halves(m_kr)
    pad = jnp.zeros((D_MODEL, LANES - GLA_GATE_RANK), w_in.dtype)
    w1 = jnp.concatenate([g_q, g_k, g_v, g_z, m_z, gate_gla, gate_mla,
                          m_cq, m_ckv, m_kr, m_kr, m_krs, m_krs, g_lr, pad], axis=1).astype(BF16)
    assert w1.shape == (D_MODEL, W_MAIN + W_SMALL)

    uq = mla_w_uq.reshape(MLA_Q_RANK, MLA_HEADS, MLA_QK)
    uq_nope = uq[:, :, :MLA_NOPE].reshape(MLA_Q_RANK, MLA_HEADS * MLA_NOPE)
    uq_rope = uq[:, :, MLA_NOPE:]
    wuq = jnp.concatenate([uq_nope, uq_rope.reshape(MLA_Q_RANK, -1),
                           _swap_halves(uq_rope).reshape(MLA_Q_RANK, -1)], axis=1).astype(BF16)

    ukv = mla_w_ukv.reshape(MLA_KV_RANK, MLA_HEADS, MLA_NOPE + MLA_DV)
    wukt = jnp.transpose(ukv[:, :, :MLA_NOPE], (1, 2, 0)).astype(BF16)
    zero = jnp.zeros_like(wukt[0::2])
    wukt = jnp.concatenate([jnp.concatenate([wukt[0::2], zero], axis=2),
                            jnp.concatenate([zero, wukt[1::2]], axis=2)], axis=1)
    wuv = jnp.transpose(ukv[:, :, MLA_NOPE:], (1, 0, 2)).astype(BF16)

    gw = jnp.concatenate([gla_gate_w, jnp.zeros((LANES - GLA_GATE_RANK, GLA_KW), gla_gate_w.dtype)],
                         axis=0).astype(BF16)
    return w1, wuq, wukt, wuv, gw


def _rope_tables(n):
    inv = 1.0 / (ROPE_BASE ** (jnp.arange(0, MLA_ROPE, 2, dtype=F32) / MLA_ROPE))
    ang = jnp.arange(n, dtype=F32)[:, None] * inv[None, :]
    cos, sin = jnp.cos(ang), jnp.sin(ang)
    cos = jnp.concatenate([cos, cos, cos, cos], axis=1)
    sin = jnp.concatenate([-sin, sin, -sin, sin], axis=1)
    return cos, sin


def kernel(x, meta_tokens, norm_g, w_in, gla_gate_w, gla_gate_b, gla_norm_g, gla_proj,
           mla_q_norm_g, mla_w_uq, mla_kv_norm_g, mla_w_ukv, mla_proj, w_out, final_norm_g):
    B, S, D = x.shape
    assert D == D_MODEL and norm_g.shape[0] == 1 and meta_tokens.shape == (N_META, D_MODEL)

    w1, wuq, wukt, wuv, gw = _prep_weights(w_in[0], gla_gate_w[0], mla_w_uq[0], mla_w_ukv[0])
    row = lambda a: a.reshape(1, -1).astype(F32)
    ng, qg, kvg = row(norm_g[0]), row(mla_q_norm_g[0]), row(mla_kv_norm_g[0])
    gb, gng, fng = row(gla_gate_b[0]), row(gla_norm_g[0]), row(final_norm_g)
    cos, sin = _rope_tables(N_META + S)

    proj_args = (ng, w1, qg, kvg, wuq, wukt)
    meta = _inproj(meta_tokens[None].astype(F32), *proj_args, cos[:N_META], sin[:N_META],
                   tm=N_META, emit_vt=False)
    _, k_m, v_m, _, _, _, _, glr_m, _, kmeta = meta
    kvtmeta = jnp.concatenate([jnp.swapaxes(kmeta[:, :, 0:MLA_KV_RANK], 1, 2),
                               jnp.ones((1, MLA_VT_ROWS - MLA_KV_RANK, N_META), BF16)], axis=1)
    q, k, v, sz, smz, sgg, sgm, glr, qm, km, kvt = _inproj(
        x, *proj_args, cos[N_META:], sin[N_META:], tm=512, emit_vt=True)

    ya = _gla(q, k, v, glr, sz, k_m, v_m, glr_m, gw, gb, gng, tc=512)
    yb = _mla(qm, km, kvt, kmeta, kvtmeta, wuv, smz, tq=256, tk=256)
    return _outproj(ya, yb, sgg, sgm, x, gla_proj[0].astype(BF16), mla_proj[0].astype(BF16),
                    w_out[0].astype(BF16), fng, tm=1024)
```

```python
import functools
import math

import jax
import jax.numpy as jnp
from jax import lax
from jax.experimental import pallas as pl
from jax.experimental.pallas import tpu as pltpu

F32 = jnp.float32
BF16 = jnp.bfloat16

D_MODEL = 1024
N_META = 16
EPS = 1e-6

GLA_HEADS = 4
GLA_DK = 128
GLA_DV = 256
GLA_GATE_RANK = 16
GLA_GATE_NORMALIZER = 16.0
GLA_CHUNK = 64
GLA_KW = GLA_HEADS * GLA_DK
GLA_VW = GLA_HEADS * GLA_DV

MLA_HEADS = 8
MLA_NOPE = 128
MLA_ROPE = 64
MLA_DV = 128
MLA_Q_RANK = 256
MLA_KV_RANK = 128
MLA_QK = MLA_NOPE + MLA_ROPE
MLA_VW = MLA_HEADS * MLA_DV
ROPE_BASE = 10000.0

SPLITS = (GLA_KW, GLA_KW, GLA_VW, GLA_GATE_RANK, GLA_VW,
          MLA_Q_RANK, MLA_KV_RANK, MLA_ROPE, MLA_VW, D_MODEL, D_MODEL)

LANES = 128
SUBLANES = 8
SOFTMAX_SLAB = 32
MLA_KPAD = 2 * LANES
MLA_VT_ROWS = MLA_KV_RANK + 16
W_SMALL = MLA_Q_RANK + MLA_KV_RANK + 3 * LANES
VMEM_LIMIT = 56 * 1024 * 1024


def _rms(x, g):
    return x * lax.rsqrt(jnp.mean(x * x, axis=-1, keepdims=True) + EPS) * g


def _dot(a, b):
    return jnp.dot(a, b, preferred_element_type=F32)


def _dot_nt(a, b):
    return lax.dot_general(a, b, (((1,), (1,)), ((), ())), preferred_element_type=F32)


def _dot_tn(a, b):
    return lax.dot_general(a, b, (((0,), (0,)), ((), ())), preferred_element_type=F32)


def _inproj_kernel(x_ref, ng_ref, wq_ref, wk_ref, wv_ref, wgz_ref, wmz_ref, wgg_ref, wgm_ref, wsmall_ref,
                   qg_ref, kvg_ref, wuq_ref, wukt_ref, cos_ref, sin_ref,
                   q_ref, k_ref, v_ref, sz_ref, smz_ref, sgg_ref, sgm_ref, glr_ref, qm_ref, km_ref,
                   *maybe_kvt_ref):
    x = x_ref[0]
    u = (x * ng_ref[...]).astype(BF16)
    inv_rms = lax.rsqrt(jnp.mean(x * x, axis=-1, keepdims=True) + EPS)

    half_inv_rms = 0.5 * inv_rms

    def proj(w_ref):
        return _dot(u, w_ref[...]) * inv_rms

    def silu_proj(w_ref):
        h = _dot(u, w_ref[...]) * half_inv_rms
        return h * jnp.tanh(h) + h

    def sigmoid_proj(w_ref):
        return 0.5 * jnp.tanh(_dot(u, w_ref[...]) * half_inv_rms) + 0.5

    small = proj(wsmall_ref)
    cq = small[:, 0:MLA_Q_RANK]
    ckv = small[:, MLA_Q_RANK:MLA_Q_RANK + MLA_KV_RANK]
    o = MLA_Q_RANK + MLA_KV_RANK
    kr = small[:, o:o + LANES]
    krs = small[:, o + LANES:o + 2 * LANES]
    glr_ref[0] = small[:, o + 2 * LANES:o + 3 * LANES].astype(BF16)

    cos = cos_ref[...]
    sin = sin_ref[...]
    ckvn = _rms(ckv, kvg_ref[...])
    km_ref[0, :, 0:LANES] = ckvn.astype(BF16)
    for kvt_ref in maybe_kvt_ref:
        kvt_ref[0, 0:MLA_KV_RANK, :] = ckvn.T.astype(BF16)
        kvt_ref[0, MLA_KV_RANK:MLA_VT_ROWS, :] = jnp.ones((MLA_VT_ROWS - MLA_KV_RANK, x.shape[0]), BF16)
    km_ref[0, :, LANES:MLA_KPAD] = (kr * cos + krs * sin).astype(BF16)
    cqn = _rms(cq, qg_ref[...]).astype(BF16)

    sz_ref[0] = silu_proj(wgz_ref).astype(BF16)
    qall = _dot(cqn, wuq_ref[...])
    smz_ref[0] = silu_proj(wmz_ref).astype(BF16)

    scale = MLA_QK ** -0.5 * math.log2(math.e)
    nope_w = MLA_HEADS * MLA_NOPE
    rope_w = MLA_HEADS * MLA_ROPE
    lane = lax.broadcasted_iota(jnp.int32, (x.shape[0], LANES), 1)
    for g in range(MLA_HEADS // 2):
        qn = qall[:, 2 * g * MLA_NOPE:(2 * g + 2) * MLA_NOPE].astype(BF16)
        qa = _dot(qn, wukt_ref[g]) * scale
        qm_ref[0, 2 * g, :, 0:LANES] = qa[:, 0:LANES].astype(BF16)
        qm_ref[0, 2 * g + 1, :, 0:LANES] = qa[:, LANES:2 * LANES].astype(BF16)
        a = qall[:, nope_w + g * LANES:nope_w + (g + 1) * LANES]
        b = qall[:, nope_w + rope_w + g * LANES:nope_w + rope_w + (g + 1) * LANES]
        r = (a * cos + b * sin) * scale
        qm_ref[0, 2 * g, :, LANES:MLA_KPAD] = jnp.where(lane < MLA_ROPE, r, 0.0).astype(BF16)
        qm_ref[0, 2 * g + 1, :, LANES:MLA_KPAD] = jnp.where(lane >= MLA_ROPE, r, 0.0).astype(BF16)

    sgg_ref[0] = sigmoid_proj(wgg_ref).astype(BF16)
    sgm_ref[0] = sigmoid_proj(wgm_ref).astype(BF16)
    q_ref[0] = (proj(wq_ref) * (GLA_DK ** -0.5)).astype(BF16)
    v_ref[0] = proj(wv_ref).astype(BF16)
    k_ref[0] = proj(wk_ref).astype(BF16)


def _inproj(x, ng, wide, qg, kvg, wuq, wukt, cos, sin, tm, emit_vt):
    B, S, _ = x.shape
    grid = (B, S // tm)
    tok = lambda w: pl.BlockSpec((1, tm, w), lambda b, i: (b, i, 0))
    full = lambda a: pl.BlockSpec(a.shape, lambda b, i: (0,) * a.ndim)
    out_shapes = [
        jax.ShapeDtypeStruct((B, S, GLA_KW), BF16),
        jax.ShapeDtypeStruct((B, S, GLA_KW), BF16),
        jax.ShapeDtypeStruct((B, S, GLA_VW), BF16),
        jax.ShapeDtypeStruct((B, S, GLA_VW), BF16),
        jax.ShapeDtypeStruct((B, S, MLA_VW), BF16),
        jax.ShapeDtypeStruct((B, S, D_MODEL), BF16),
        jax.ShapeDtypeStruct((B, S, D_MODEL), BF16),
        jax.ShapeDtypeStruct((B, S, LANES), BF16),
        jax.ShapeDtypeStruct((B, MLA_HEADS, S, MLA_KPAD), BF16),
        jax.ShapeDtypeStruct((B, S, MLA_KPAD), BF16),
    ]
    out_specs = [tok(GLA_KW), tok(GLA_KW), tok(GLA_VW), tok(GLA_VW), tok(MLA_VW),
                 tok(D_MODEL), tok(D_MODEL), tok(LANES),
                 pl.BlockSpec((1, MLA_HEADS, tm, MLA_KPAD), lambda b, i: (b, 0, i, 0)),
                 tok(MLA_KPAD)]
    if emit_vt:
        out_shapes.append(jax.ShapeDtypeStruct((B, MLA_VT_ROWS, S), BF16))
        out_specs.append(pl.BlockSpec((1, MLA_VT_ROWS, tm), lambda b, i: (b, 0, i)))
    tab = pl.BlockSpec((tm, LANES), lambda b, i: (i, 0))
    return pl.pallas_call(
        _inproj_kernel,
        grid=grid,
        in_specs=[tok(D_MODEL), full(ng), *[full(w) for w in wide],
                  full(qg), full(kvg), full(wuq), full(wukt), tab, tab],
        out_specs=out_specs,
        out_shape=out_shapes,
        compiler_params=pltpu.CompilerParams(
            dimension_semantics=("parallel", "parallel"), vmem_limit_bytes=VMEM_LIMIT),
        name="inproj",
    )(x, ng, *wide, qg, kvg, wuq, wukt, cos, sin)


def _tri(n):
    row = lax.broadcasted_iota(jnp.int32, (n, n), 0)
    col = lax.broadcasted_iota(jnp.int32, (n, n), 1)
    return col <= row


def _gla_log_gates(glr, gw_ref, gb_ref):
    return jax.nn.log_sigmoid(_dot(glr, gw_ref[...]) + gb_ref[...]) / GLA_GATE_NORMALIZER


def _cumsum_rows(tri, g):
    hi = g.astype(BF16)
    lo = (g - hi.astype(F32)).astype(BF16)
    return _dot(tri, hi) + _dot(tri, lo)


def _gla_kernel(q_ref, k_ref, v_ref, glr_ref, sz_ref, km_ref, vm_ref, glrm_ref,
                gw_ref, gb_ref, gng_ref, o_ref, st_ref, oi_ref, u_ref, *, n_chunks):
    C = GLA_CHUNK

    @pl.when(pl.program_id(1) == 0)
    def _init_state():
        b = _cumsum_rows(_tri(N_META).astype(BF16), _gla_log_gates(glrm_ref[0], gw_ref, gb_ref))
        kl = (km_ref[0].astype(F32) * jnp.exp(b[-1:, :] - b)).astype(BF16)
        for h in range(GLA_HEADS):
            st_ref[h] = _dot_tn(vm_ref[0, :, h * GLA_DV:(h + 1) * GLA_DV],
                                kl[:, h * GLA_DK:(h + 1) * GLA_DK])

    causal = _tri(C)
    tri = causal.astype(BF16)
    gng = gng_ref[...]
    heads = [(slice(h * GLA_DK, (h + 1) * GLA_DK), slice(h * GLA_DV, (h + 1) * GLA_DV))
             for h in range(GLA_HEADS)]
    chunks = [slice(c * C, (c + 1) * C) for c in range(n_chunks)]

    g = _gla_log_gates(glr_ref[0], gw_ref, gb_ref)
    b = jnp.concatenate([_cumsum_rows(tri, g[rows]) for rows in chunks], axis=0)
    b3 = b.reshape(n_chunks, C, GLA_KW)
    b_last = b3[:, C - 1:C, :]
    q = q_ref[0].astype(F32)
    k = k_ref[0].astype(F32)
    qe = (q * jnp.exp(b)).astype(BF16)
    ke = (k * jnp.exp(-b)).astype(BF16)
    kl = (k * jnp.exp(b_last - b3).reshape(n_chunks * C, GLA_KW)).astype(BF16)
    decay = jnp.exp(b_last)

    blocks = [(c, rows, h, ks, vs) for c, rows in enumerate(chunks) for h, (ks, vs) in enumerate(heads)]
    a = [jnp.where(causal, _dot_nt(qe[rows, ks], ke[rows, ks]), 0.0).astype(BF16)
         for _, rows, _, ks, _ in blocks]
    for c, rows, h, ks, vs in blocks:
        u_ref[c, h] = _dot_tn(v_ref[0, rows, vs], kl[rows, ks])
    for a_ch, (_, rows, _, _, vs) in zip(a, blocks):
        oi_ref[rows, vs] = _dot(a_ch, v_ref[0, rows, vs])

    for c, rows in enumerate(chunks):
        for h, (ks, vs) in enumerate(heads):
            st = st_ref[h]
            o = oi_ref[rows, vs] + _dot_nt(qe[rows, ks], st.astype(BF16))
            st_ref[h] = st * decay[c, :, ks] + u_ref[c, h]
            o = _rms(o, gng) * sz_ref[0, rows, vs].astype(F32)
            o_ref[0, rows, vs] = o.astype(BF16)


def _gla(q, k, v, glr, sz, km, vm, glrm, gw, gb, gng, tc):
    B, S, _ = q.shape
    grid = (B, S // tc)
    tok = lambda w: pl.BlockSpec((1, tc, w), lambda b, i: (b, i, 0))
    full = lambda a: pl.BlockSpec(a.shape, lambda b, i: (0,) * a.ndim)
    return pl.pallas_call(
        functools.partial(_gla_kernel, n_chunks=tc // GLA_CHUNK),
        grid=grid,
        in_specs=[tok(GLA_KW), tok(GLA_KW), tok(GLA_VW), tok(LANES), tok(GLA_VW),
                  full(km), full(vm), full(glrm), full(gw), full(gb), full(gng)],
        out_specs=tok(GLA_VW),
        out_shape=jax.ShapeDtypeStruct((B, S, GLA_VW), BF16),
        scratch_shapes=[pltpu.VMEM((GLA_HEADS, GLA_DV, GLA_DK), F32),
                        pltpu.VMEM((tc, GLA_VW), F32),
                        pltpu.VMEM((tc // GLA_CHUNK, GLA_HEADS, GLA_DV, GLA_DK), F32)],
        compiler_params=pltpu.CompilerParams(
            dimension_semantics=("parallel", "arbitrary"), vmem_limit_bytes=VMEM_LIMIT),
        name="gla",
    )(q, k, v, glr, sz, km, vm, glrm, gw, gb, gng)


def _mla_kernel(qm_ref, km_ref, kvt_ref, kmeta_ref, kvtmeta_ref, wuv_ref, smz_ref, o_ref,
                m_ref, l_ref, alpha_ref, acc_ref, s_ref, p_ref, bmax_ref, *, tq, tk, max_full):
    i = pl.program_id(1)
    cols = MLA_HEADS * tq
    n_full = (i * tq) // tk

    gw = MLA_KPAD
    hpg = gw // tq
    groups = [(g, slice(g * gw, (g + 1) * gw)) for g in range(cols // gw)]

    def scores(kb, g):
        return _dot_nt(kb, qm_ref[0, g * hpg:(g + 1) * hpg].reshape(gw, MLA_KPAD))

    def keys(k0):
        return km_ref[0, k0:k0 + tk, :]

    def values(k0):
        return kvt_ref[0, :, k0:k0 + tk]

    def add_values(vt, p, cs):
        r = _dot(vt, p)
        alpha = alpha_ref[:, cs]
        acc_ref[:, cs] = alpha * acc_ref[:, cs] + r[0:MLA_KV_RANK]
        l_ref[:, cs] = alpha * l_ref[:, cs] + r[MLA_KV_RANK:MLA_KV_RANK + 1]

    def fold(x, op):
        return functools.reduce(op, [x[r:r + SUBLANES] for r in range(0, x.shape[0], SUBLANES)])

    def put_scores(buf, n, cs, s):
        s_ref[buf, 0:n, cs] = s
        bmax_ref[buf, :, cs] = fold(s, jnp.maximum)

    def softmax(buf, n, cs):
        slab = min(n, SOFTMAX_SLAB)
        m_old = m_ref[:, cs]
        m_new = jnp.maximum(m_old, jnp.max(bmax_ref[buf, :, cs], axis=0, keepdims=True))
        alpha = jnp.exp2(m_old - m_new)
        for r in range(0, n, slab):
            p_ref[buf, r:r + slab, cs] = jnp.exp2(s_ref[buf, r:r + slab, cs] - m_new).astype(BF16)
        m_ref[:, cs] = m_new
        alpha_ref[:, cs] = alpha

    def run(nf):
        diag0 = nf * tk
        block_start = lambda t: diag0 if t == 0 else (t - 1) * tk

        m_ref[...] = jnp.full(m_ref.shape, -jnp.inf, F32)
        l_ref[...] = jnp.zeros(l_ref.shape, F32)
        alpha_ref[...] = jnp.zeros(alpha_ref.shape, F32)
        acc_ref[...] = jnp.zeros(acc_ref.shape, F32)

        ktok = diag0 + lax.broadcasted_iota(jnp.int32, (tk, gw), 0)
        qtok = i * tq + (lax.broadcasted_iota(jnp.int32, (tk, gw), 1) & (tq - 1))
        for g, cs in groups:
            put_scores(0, tk, cs, jnp.where(ktok <= qtok, scores(keys(diag0), g), -jnp.inf))

        for t in range(nf + 1):
            cur, nxt = t % 2, 1 - t % 2
            for g, cs in groups:
                if t > 0:
                    add_values(values(block_start(t - 1)), p_ref[nxt, :, cs], cs)
                if t < nf:
                    put_scores(nxt, tk, cs, scores(keys(block_start(t + 1)), g))
                else:
                    put_scores(nxt, N_META, cs, scores(kmeta_ref[0], g))
                softmax(cur, tk, cs)
        last, meta = nf % 2, 1 - nf % 2
        for g, cs in groups:
            add_values(values(block_start(nf)), p_ref[last, :, cs], cs)
            softmax(meta, N_META, cs)
            add_values(kvtmeta_ref[0], p_ref[meta, 0:N_META, cs], cs)

        o_lat = (acc_ref[...] / l_ref[...]).astype(BF16)
        for h in range(MLA_HEADS):
            vs = slice(h * MLA_DV, (h + 1) * MLA_DV)
            o = _dot_tn(o_lat[:, h * tq:(h + 1) * tq], wuv_ref[h])
            o_ref[0, :, vs] = (o * smz_ref[0, :, vs].astype(F32)).astype(BF16)

    for nf in range(max_full + 1):
        pl.when(n_full == nf)(functools.partial(run, nf))


def _mla(qm, km, kvt, kmeta, kvtmeta, wuv, smz, tq, tk):
    B, H, S, _ = qm.shape
    assert tq & (tq - 1) == 0 and S % tk == 0 and tk % tq == 0
    grid = (B, S // tq)
    full = lambda a: pl.BlockSpec(a.shape, lambda b, i: (0,) * a.ndim)
    rows = H * tq
    return pl.pallas_call(
        functools.partial(_mla_kernel, tq=tq, tk=tk, max_full=(S - tq) // tk),
        grid=grid,
        in_specs=[pl.BlockSpec((1, H, tq, MLA_KPAD), lambda b, i: (b, 0, i, 0)),
                  pl.BlockSpec((1, S, MLA_KPAD), lambda b, i: (b, 0, 0)),
                  pl.BlockSpec((1, MLA_VT_ROWS, S), lambda b, i: (b, 0, 0)),
                  full(kmeta), full(kvtmeta), full(wuv),
                  pl.BlockSpec((1, tq, MLA_VW), lambda b, i: (b, i, 0))],
        out_specs=pl.BlockSpec((1, tq, MLA_VW), lambda b, i: (b, i, 0)),
        out_shape=jax.ShapeDtypeStruct((B, S, MLA_VW), BF16),
        scratch_shapes=[pltpu.VMEM((1, rows), F32), pltpu.VMEM((1, rows), F32),
                        pltpu.VMEM((1, rows), F32),
                        pltpu.VMEM((MLA_KV_RANK, rows), F32),
                        pltpu.VMEM((2, tk, rows), F32), pltpu.VMEM((2, tk, rows), BF16),
                        pltpu.VMEM((2, SUBLANES, rows), F32)],
        compiler_params=pltpu.CompilerParams(
            dimension_semantics=("parallel", "arbitrary"), vmem_limit_bytes=VMEM_LIMIT),
        name="mla",
    )(qm, km, kvt, kmeta, kvtmeta, wuv, smz)


def _outproj_kernel(ya_ref, yb_ref, sgg_ref, sgm_ref, x_ref, gp_ref, mp_ref, wo_ref, fng_ref, o_ref):
    y_a = _dot(ya_ref[0], gp_ref[...])
    y_b = _dot(yb_ref[0], mp_ref[...])
    merged = sgg_ref[0].astype(F32) * y_a + sgm_ref[0].astype(F32) * y_b
    h = x_ref[0] + _dot(merged.astype(BF16), wo_ref[...])
    o_ref[0] = _rms(h, fng_ref[...])


def _outproj(ya, yb, sgg, sgm, x, gp, mp, wo, fng, tm):
    B, S, _ = x.shape
    grid = (B, S // tm)
    tok = pl.BlockSpec((1, tm, D_MODEL), lambda b, i: (b, i, 0))
    full = lambda a: pl.BlockSpec(a.shape, lambda b, i: (0,) * a.ndim)
    return pl.pallas_call(
        _outproj_kernel,
        grid=grid,
        in_specs=[tok, tok, tok, tok, tok, full(gp), full(mp), full(wo), full(fng)],
        out_specs=tok,
        out_shape=jax.ShapeDtypeStruct((B, S, D_MODEL), F32),
        compiler_params=pltpu.CompilerParams(
            dimension_semantics=("parallel", "parallel"), vmem_limit_bytes=VMEM_LIMIT),
        name="outproj",
    )(ya, yb, sgg, sgm, x, gp, mp, wo, fng)


def _swap_halves(w):
    half = w.shape[-1] // 2
    return jnp.concatenate([w[..., half:], w[..., :half]], axis=-1)


def _prep_weights(w_in, gla_gate_w, mla_w_uq, mla_w_ukv):
    cuts = [0]
    for s in SPLITS:
        cuts.append(cuts[-1] + s)
    (g_q, g_k, g_v, g_lr, g_z, m_cq, m_ckv, m_kr, m_z, gate_gla, gate_mla) = [
        w_in[:, cuts[n]:cuts[n + 1]] for n in range(len(SPLITS))]
    m_krs = _swap_halves(m_kr)
    pad = jnp.zeros((D_MODEL, LANES - GLA_GATE_RANK), w_in.dtype)
    small = jnp.concatenate([m_cq, m_ckv, m_kr, m_kr, m_krs, m_krs, g_lr, pad], axis=1)
    assert small.shape == (D_MODEL, W_SMALL)
    wide = tuple(w.astype(BF16) for w in (g_q, g_k, g_v, g_z, m_z, gate_gla, gate_mla, small))

    uq = mla_w_uq.reshape(MLA_Q_RANK, MLA_HEADS, MLA_QK)
    uq_nope = uq[:, :, :MLA_NOPE].reshape(MLA_Q_RANK, MLA_HEADS * MLA_NOPE)
    uq_rope = uq[:, :, MLA_NOPE:]
    wuq = jnp.concatenate([uq_nope, uq_rope.reshape(MLA_Q_RANK, -1),
                           _swap_halves(uq_rope).reshape(MLA_Q_RANK, -1)], axis=1).astype(BF16)

    ukv = mla_w_ukv.reshape(MLA_KV_RANK, MLA_HEADS, MLA_NOPE + MLA_DV)
    wukt = jnp.transpose(ukv[:, :, :MLA_NOPE], (1, 2, 0)).astype(BF16)
    zero = jnp.zeros_like(wukt[0::2])
    wukt = jnp.concatenate([jnp.concatenate([wukt[0::2], zero], axis=2),
                            jnp.concatenate([zero, wukt[1::2]], axis=2)], axis=1)
    wuv = jnp.transpose(ukv[:, :, MLA_NOPE:], (1, 0, 2)).astype(BF16)

    gw = jnp.concatenate([gla_gate_w, jnp.zeros((LANES - GLA_GATE_RANK, GLA_KW), gla_gate_w.dtype)],
                         axis=0).astype(BF16)
    return wide, wuq, wukt, wuv, gw


def _rope_tables(n):
    inv = 1.0 / (ROPE_BASE ** (jnp.arange(0, MLA_ROPE, 2, dtype=F32) / MLA_ROPE))
    ang = jnp.arange(n, dtype=F32)[:, None] * inv[None, :]
    cos, sin = jnp.cos(ang), jnp.sin(ang)
    cos = jnp.concatenate([cos, cos, cos, cos], axis=1)
    sin = jnp.concatenate([-sin, sin, -sin, sin], axis=1)
    return cos, sin


def kernel(x, meta_tokens, norm_g, w_in, gla_gate_w, gla_gate_b, gla_norm_g, gla_proj,
           mla_q_norm_g, mla_w_uq, mla_kv_norm_g, mla_w_ukv, mla_proj, w_out, final_norm_g):
    B, S, D = x.shape
    assert D == D_MODEL and norm_g.shape[0] == 1 and meta_tokens.shape == (N_META, D_MODEL)

    wide, wuq, wukt, wuv, gw = _prep_weights(w_in[0], gla_gate_w[0], mla_w_uq[0], mla_w_ukv[0])
    row = lambda a: a.reshape(1, -1).astype(F32)
    ng, qg, kvg = row(norm_g[0]), row(mla_q_norm_g[0]), row(mla_kv_norm_g[0])
    gb, gng, fng = row(gla_gate_b[0]), row(gla_norm_g[0]), row(final_norm_g)
    cos, sin = _rope_tables(N_META + S)

    proj_args = (ng, wide, qg, kvg, wuq, wukt)
    meta = _inproj(meta_tokens[None].astype(F32), *proj_args, cos[:N_META], sin[:N_META],
                   tm=N_META, emit_vt=False)
    _, k_m, v_m, _, _, _, _, glr_m, _, kmeta = meta
    kvtmeta = jnp.concatenate([jnp.swapaxes(kmeta[:, :, 0:MLA_KV_RANK], 1, 2),
                               jnp.ones((1, MLA_VT_ROWS - MLA_KV_RANK, N_META), BF16)], axis=1)
    q, k, v, sz, smz, sgg, sgm, glr, qm, km, kvt = _inproj(
        x, *proj_args, cos[N_META:], sin[N_META:], tm=512, emit_vt=True)

    ya = _gla(q, k, v, glr, sz, k_m, v_m, glr_m, gw, gb, gng, tc=512)
    yb = _mla(qm, km, kvt, kmeta, kvtmeta, wuv, smz, tq=256, tk=256)
    return _outproj(ya, yb, sgg, sgm, x, gla_proj[0].astype(BF16), mla_proj[0].astype(BF16),
                    w_out[0].astype(BF16), fng, tm=1024)
```

```python
import functools
import math

import jax
import jax.numpy as jnp
from jax import lax
from jax.experimental import pallas as pl
from jax.experimental.pallas import tpu as pltpu

F32 = jnp.float32
BF16 = jnp.bfloat16

D_MODEL = 1024
N_META = 16
EPS = 1e-6

GLA_HEADS = 4
GLA_DK = 128
GLA_DV = 256
GLA_GATE_RANK = 16
GLA_GATE_NORMALIZER = 16.0
GLA_CHUNK = 64
GLA_KW = GLA_HEADS * GLA_DK
GLA_VW = GLA_HEADS * GLA_DV

MLA_HEADS = 8
MLA_NOPE = 128
MLA_ROPE = 64
MLA_DV = 128
MLA_Q_RANK = 256
MLA_KV_RANK = 128
MLA_QK = MLA_NOPE + MLA_ROPE
MLA_VW = MLA_HEADS * MLA_DV
ROPE_BASE = 10000.0

SPLITS = (GLA_KW, GLA_KW, GLA_VW, GLA_GATE_RANK, GLA_VW,
          MLA_Q_RANK, MLA_KV_RANK, MLA_ROPE, MLA_VW, D_MODEL, D_MODEL)

LANES = 128
SUBLANES = 8
SOFTMAX_SLAB = 32
MXU_TILE = 256
MLA_KPAD = 2 * LANES
MLA_VT_ROWS = MLA_KV_RANK + 16
W_SMALL = MLA_Q_RANK + MLA_KV_RANK + 3 * LANES
VMEM_LIMIT = 56 * 1024 * 1024


def _rms(x, g):
    return x * lax.rsqrt(jnp.mean(x * x, axis=-1, keepdims=True) + EPS) * g


def _dot(a, b):
    return jnp.dot(a, b, preferred_element_type=F32)


def _dot_nt(a, b):
    return lax.dot_general(a, b, (((1,), (1,)), ((), ())), preferred_element_type=F32)


def _dot_tn(a, b):
    return lax.dot_general(a, b, (((0,), (0,)), ((), ())), preferred_element_type=F32)


def _inproj_kernel(x_ref, ng_ref, wq_ref, wk_ref, wv_ref, wgz_ref, wmz_ref, wgg_ref, wgm_ref, wsmall_ref,
                   qg_ref, kvg_ref, wuq_ref, wukt_ref, cos_ref, sin_ref,
                   q_ref, k_ref, v_ref, sz_ref, smz_ref, sgg_ref, sgm_ref, glr_ref, qm_ref, km_ref,
                   *maybe_kvt_ref):
    x = x_ref[0]
    u = (x * ng_ref[...]).astype(BF16)
    inv_rms = lax.rsqrt(jnp.mean(x * x, axis=-1, keepdims=True) + EPS)

    half_inv_rms = 0.5 * inv_rms

    def proj(w_ref):
        return _dot(u, w_ref[...]) * inv_rms

    def silu_proj(w_ref):
        h = _dot(u, w_ref[...]) * half_inv_rms
        return h * jnp.tanh(h) + h

    def sigmoid_proj(w_ref):
        return 0.5 * jnp.tanh(_dot(u, w_ref[...]) * half_inv_rms) + 0.5

    small = proj(wsmall_ref)
    cq = small[:, 0:MLA_Q_RANK]
    ckv = small[:, MLA_Q_RANK:MLA_Q_RANK + MLA_KV_RANK]
    o = MLA_Q_RANK + MLA_KV_RANK
    kr = small[:, o:o + LANES]
    krs = small[:, o + LANES:o + 2 * LANES]
    glr_ref[0] = small[:, o + 2 * LANES:o + 3 * LANES].astype(BF16)

    cos = cos_ref[...]
    sin = sin_ref[...]
    ckvn = _rms(ckv, kvg_ref[...])
    km_ref[0, :, 0:LANES] = ckvn.astype(BF16)
    for kvt_ref in maybe_kvt_ref:
        kvt_ref[0, 0:MLA_KV_RANK, :] = ckvn.T.astype(BF16)
        kvt_ref[0, MLA_KV_RANK:MLA_VT_ROWS, :] = jnp.ones((MLA_VT_ROWS - MLA_KV_RANK, x.shape[0]), BF16)
    km_ref[0, :, LANES:MLA_KPAD] = (kr * cos + krs * sin).astype(BF16)
    cqn = _rms(cq, qg_ref[...]).astype(BF16)

    sz_ref[0] = silu_proj(wgz_ref).astype(BF16)
    qall = _dot(cqn, wuq_ref[...])
    smz_ref[0] = silu_proj(wmz_ref).astype(BF16)

    scale = MLA_QK ** -0.5 * math.log2(math.e)
    nope_w = MLA_HEADS * MLA_NOPE
    rope_w = MLA_HEADS * MLA_ROPE
    lane = lax.broadcasted_iota(jnp.int32, (x.shape[0], LANES), 1)
    for g in range(MLA_HEADS // 2):
        qn = qall[:, 2 * g * MLA_NOPE:(2 * g + 2) * MLA_NOPE].astype(BF16)
        qa = _dot(qn, wukt_ref[g]) * scale
        qm_ref[0, 2 * g, :, 0:LANES] = qa[:, 0:LANES].astype(BF16)
        qm_ref[0, 2 * g + 1, :, 0:LANES] = qa[:, LANES:2 * LANES].astype(BF16)
        a = qall[:, nope_w + g * LANES:nope_w + (g + 1) * LANES]
        b = qall[:, nope_w + rope_w + g * LANES:nope_w + rope_w + (g + 1) * LANES]
        r = (a * cos + b * sin) * scale
        qm_ref[0, 2 * g, :, LANES:MLA_KPAD] = jnp.where(lane < MLA_ROPE, r, 0.0).astype(BF16)
        qm_ref[0, 2 * g + 1, :, LANES:MLA_KPAD] = jnp.where(lane >= MLA_ROPE, r, 0.0).astype(BF16)

    sgg_ref[0] = sigmoid_proj(wgg_ref).astype(BF16)
    sgm_ref[0] = sigmoid_proj(wgm_ref).astype(BF16)
    q_ref[0] = (proj(wq_ref) * (GLA_DK ** -0.5)).astype(BF16)
    v_ref[0] = proj(wv_ref).astype(BF16)
    k_ref[0] = proj(wk_ref).astype(BF16)


def _inproj(x, ng, wide, qg, kvg, wuq, wukt, cos, sin, tm, emit_vt):
    B, S, _ = x.shape
    grid = (B, S // tm)
    tok = lambda w: pl.BlockSpec((1, tm, w), lambda b, i: (b, i, 0))
    full = lambda a: pl.BlockSpec(a.shape, lambda b, i: (0,) * a.ndim)
    out_shapes = [
        jax.ShapeDtypeStruct((B, S, GLA_KW), BF16),
        jax.ShapeDtypeStruct((B, S, GLA_KW), BF16),
        jax.ShapeDtypeStruct((B, S, GLA_VW), BF16),
        jax.ShapeDtypeStruct((B, S, GLA_VW), BF16),
        jax.ShapeDtypeStruct((B, S, MLA_VW), BF16),
        jax.ShapeDtypeStruct((B, S, D_MODEL), BF16),
        jax.ShapeDtypeStruct((B, S, D_MODEL), BF16),
        jax.ShapeDtypeStruct((B, S, LANES), BF16),
        jax.ShapeDtypeStruct((B, MLA_HEADS, S, MLA_KPAD), BF16),
        jax.ShapeDtypeStruct((B, S, MLA_KPAD), BF16),
    ]
    out_specs = [tok(GLA_KW), tok(GLA_KW), tok(GLA_VW), tok(GLA_VW), tok(MLA_VW),
                 tok(D_MODEL), tok(D_MODEL), tok(LANES),
                 pl.BlockSpec((1, MLA_HEADS, tm, MLA_KPAD), lambda b, i: (b, 0, i, 0)),
                 tok(MLA_KPAD)]
    if emit_vt:
        out_shapes.append(jax.ShapeDtypeStruct((B, MLA_VT_ROWS, S), BF16))
        out_specs.append(pl.BlockSpec((1, MLA_VT_ROWS, tm), lambda b, i: (b, 0, i)))
    tab = pl.BlockSpec((tm, LANES), lambda b, i: (i, 0))
    return pl.pallas_call(
        _inproj_kernel,
        grid=grid,
        in_specs=[tok(D_MODEL), full(ng), *[full(w) for w in wide],
                  full(qg), full(kvg), full(wuq), full(wukt), tab, tab],
        out_specs=out_specs,
        out_shape=out_shapes,
        compiler_params=pltpu.CompilerParams(
            dimension_semantics=("parallel", "parallel"), vmem_limit_bytes=VMEM_LIMIT),
        name="inproj",
    )(x, ng, *wide, qg, kvg, wuq, wukt, cos, sin)


def _tri(n):
    row = lax.broadcasted_iota(jnp.int32, (n, n), 0)
    col = lax.broadcasted_iota(jnp.int32, (n, n), 1)
    return col <= row


def _gla_log_gates(glr, gw_ref, gb_ref):
    return jax.nn.log_sigmoid(_dot(glr, gw_ref[...]) + gb_ref[...]) / GLA_GATE_NORMALIZER


def _cumsum_rows(tri, g):
    hi = g.astype(BF16)
    lo = (g - hi.astype(F32)).astype(BF16)
    return _dot(tri, hi) + _dot(tri, lo)


def _tail_kernel(q_ref, k_ref, v_ref, glr_ref, sz_ref, km_ref, vm_ref, glrm_ref, gw_ref, gb_ref, gng_ref,
                 yb_ref, sgg_ref, sgm_ref, x_ref, gp_ref, mp_ref, wo_ref, fng_ref,
                 o_ref, st_ref, oi_ref, u_ref, ya_ref, yacc_ref, merged_ref, *, n_chunks, tiles_per_row):
    C = GLA_CHUNK
    step = pl.program_id(0)
    n_tiles = pl.num_programs(0) - 1
    tile_in_row = jnp.minimum(step, n_tiles - 1) % tiles_per_row

    @pl.when(step == 0)
    def _no_previous_tile():
        ya_ref[...] = jnp.zeros(ya_ref.shape, BF16)

    @pl.when(tile_in_row == 0)
    def _init_state():
        b = _cumsum_rows(_tri(N_META).astype(BF16), _gla_log_gates(glrm_ref[0], gw_ref, gb_ref))
        kl = (km_ref[0].astype(F32) * jnp.exp(b[-1:, :] - b)).astype(BF16)
        for h in range(GLA_HEADS):
            st_ref[h] = _dot_tn(vm_ref[0, :, h * GLA_DV:(h + 1) * GLA_DV],
                                kl[:, h * GLA_DK:(h + 1) * GLA_DK])

    causal = _tri(C)
    tri = causal.astype(BF16)
    gng = gng_ref[...]
    heads = [(slice(h * GLA_DK, (h + 1) * GLA_DK), slice(h * GLA_DV, (h + 1) * GLA_DV))
             for h in range(GLA_HEADS)]
    chunks = [slice(c * C, (c + 1) * C) for c in range(n_chunks)]
    col_blocks = [slice(c, c + MXU_TILE) for c in range(0, D_MODEL, MXU_TILE)]

    def merge_block(cs, rs):
        y_b = _dot(yb_ref[0, rs, :], mp_ref[:, cs])
        merged = (sgg_ref[0, rs, cs].astype(F32) * yacc_ref[rs, cs]
                  + sgm_ref[0, rs, cs].astype(F32) * y_b)
        merged_ref[rs, cs] = merged.astype(BF16)

    def out_block(cs, rs):
        h = x_ref[0, rs, cs] + _dot(merged_ref[rs, :], wo_ref[:, cs])
        o_ref[0, rs, cs] = h
        return jnp.sum(h * h, axis=-1, keepdims=True)

    gate_logits = _dot(glr_ref[0], gw_ref[...]) + gb_ref[...]
    for cs in col_blocks:
        yacc_ref[:, cs] = _dot(ya_ref[...], gp_ref[:, cs])

    g = jax.nn.log_sigmoid(gate_logits) / GLA_GATE_NORMALIZER
    b = jnp.concatenate([_cumsum_rows(tri, g[rows]) for rows in chunks], axis=0)
    b3 = b.reshape(n_chunks, C, GLA_KW)
    b_last = b3[:, C - 1:C, :]
    q = q_ref[0].astype(F32)
    k = k_ref[0].astype(F32)
    qe = (q * jnp.exp(b)).astype(BF16)
    ke = (k * jnp.exp(-b)).astype(BF16)
    kl = (k * jnp.exp(b_last - b3).reshape(n_chunks * C, GLA_KW)).astype(BF16)
    decay = jnp.exp(b_last)

    for cs in col_blocks[:-1]:
        merge_block(cs, slice(None))

    blocks = [(c, rows, h, ks, vs) for c, rows in enumerate(chunks) for h, (ks, vs) in enumerate(heads)]
    a = [jnp.where(causal, _dot_nt(qe[rows, ks], ke[rows, ks]), 0.0).astype(BF16)
         for _, rows, _, ks, _ in blocks]
    for c, rows, h, ks, vs in blocks:
        u_ref[c, h] = _dot_tn(v_ref[0, rows, vs], kl[rows, ks])
    for a_ch, (_, rows, _, _, vs) in zip(a, blocks):
        oi_ref[rows, vs] = _dot(a_ch, v_ref[0, rows, vs])

    def gla_chunk(c):
        rows = chunks[c]
        for h, (ks, vs) in enumerate(heads):
            st = st_ref[h]
            o = oi_ref[rows, vs] + _dot_nt(qe[rows, ks], st.astype(BF16))
            st_ref[h] = st * decay[c, :, ks] + u_ref[c, h]
            o = _rms(o, gng) * sz_ref[0, rows, vs].astype(F32)
            ya_ref[rows, vs] = o.astype(BF16)

    half = n_chunks * C // 2
    halves = [slice(0, half), slice(half, 2 * half)]
    work = [functools.partial(merge_block, col_blocks[-1], rs) for rs in halves]
    work += [functools.partial(out_block, cs, rs) for cs in col_blocks for rs in halves]
    results = [w() for w in work[:len(work) - n_chunks]]
    for c, w in enumerate(work[len(work) - n_chunks:]):
        results.append(w())
        gla_chunk(c)
    sumsq = [functools.reduce(jnp.add, results[2 + r::2]) for r in range(2)]
    for rs, ss in zip(halves, sumsq):
        o_ref[0, rs, :] = o_ref[0, rs, :] * lax.rsqrt(ss / D_MODEL + EPS) * fng_ref[...]


def _tail(q, k, v, glr, sz, km, vm, glrm, gw, gb, gng, yb, sgg, sgm, x, gp, mp, wo, fng, tc):
    B, S, _ = q.shape
    n = S // tc
    n_tiles = B * n
    gla_tile = lambda s: jnp.minimum(s, n_tiles - 1)
    out_tile = lambda s: jnp.maximum(s - 1, 0)
    gtok = lambda w: pl.BlockSpec((1, tc, w), lambda s: (gla_tile(s) // n, gla_tile(s) % n, 0))
    otok = pl.BlockSpec((1, tc, D_MODEL), lambda s: (out_tile(s) // n, out_tile(s) % n, 0))
    full = lambda a: pl.BlockSpec(a.shape, lambda s: (0,) * a.ndim)
    n_chunks = tc // GLA_CHUNK
    return pl.pallas_call(
        functools.partial(_tail_kernel, n_chunks=n_chunks, tiles_per_row=n),
        grid=(n_tiles + 1,),
        in_specs=[gtok(GLA_KW), gtok(GLA_KW), gtok(GLA_VW), gtok(LANES), gtok(GLA_VW),
                  full(km), full(vm), full(glrm), full(gw), full(gb), full(gng),
                  otok, otok, otok, otok, full(gp), full(mp), full(wo), full(fng)],
        out_specs=otok,
        out_shape=jax.ShapeDtypeStruct((B, S, D_MODEL), F32),
        scratch_shapes=[pltpu.VMEM((GLA_HEADS, GLA_DV, GLA_DK), F32),
                        pltpu.VMEM((tc, GLA_VW), F32),
                        pltpu.VMEM((n_chunks, GLA_HEADS, GLA_DV, GLA_DK), F32),
                        pltpu.VMEM((tc, GLA_VW), BF16),
                        pltpu.VMEM((tc, D_MODEL), F32),
                        pltpu.VMEM((tc, D_MODEL), BF16)],
        compiler_params=pltpu.CompilerParams(
            dimension_semantics=("arbitrary",), vmem_limit_bytes=VMEM_LIMIT),
        name="tail",
    )(q, k, v, glr, sz, km, vm, glrm, gw, gb, gng, yb, sgg, sgm, x, gp, mp, wo, fng)


def _mla_kernel(qm_ref, km_ref, kvt_ref, kmeta_ref, kvtmeta_ref, wuv_ref, smz_ref, o_ref,
                m_ref, l_ref, alpha_ref, acc_ref, s_ref, p_ref, bmax_ref, *, tq, tk, max_full):
    i = pl.program_id(1)
    cols = MLA_HEADS * tq
    n_full = (i * tq) // tk

    gw = MLA_KPAD
    hpg = gw // tq
    groups = [(g, slice(g * gw, (g + 1) * gw)) for g in range(cols // gw)]

    def scores(kb, g):
        return _dot_nt(kb, qm_ref[0, g * hpg:(g + 1) * hpg].reshape(gw, MLA_KPAD))

    def keys(k0):
        return km_ref[0, k0:k0 + tk, :]

    def values(k0):
        return kvt_ref[0, :, k0:k0 + tk]

    def add_values(vt, p, cs):
        r = _dot(vt, p)
        alpha = alpha_ref[:, cs]
        acc_ref[:, cs] = alpha * acc_ref[:, cs] + r[0:MLA_KV_RANK]
        l_ref[:, cs] = alpha * l_ref[:, cs] + r[MLA_KV_RANK:MLA_KV_RANK + 1]

    def fold(x, op):
        return functools.reduce(op, [x[r:r + SUBLANES] for r in range(0, x.shape[0], SUBLANES)])

    def put_scores(buf, n, cs, s):
        s_ref[buf, 0:n, cs] = s
        bmax_ref[buf, :, cs] = fold(s, jnp.maximum)

    def softmax(buf, n, cs):
        slab = min(n, SOFTMAX_SLAB)
        m_old = m_ref[:, cs]
        m_new = jnp.maximum(m_old, jnp.max(bmax_ref[buf, :, cs], axis=0, keepdims=True))
        alpha = jnp.exp2(m_old - m_new)
        for r in range(0, n, slab):
            p_ref[buf, r:r + slab, cs] = jnp.exp2(s_ref[buf, r:r + slab, cs] - m_new).astype(BF16)
        m_ref[:, cs] = m_new
        alpha_ref[:, cs] = alpha

    def run(nf):
        diag0 = nf * tk
        block_start = lambda t: diag0 if t == 0 else (t - 1) * tk

        m_ref[...] = jnp.full(m_ref.shape, -jnp.inf, F32)
        l_ref[...] = jnp.zeros(l_ref.shape, F32)
        alpha_ref[...] = jnp.zeros(alpha_ref.shape, F32)
        acc_ref[...] = jnp.zeros(acc_ref.shape, F32)

        ktok = diag0 + lax.broadcasted_iota(jnp.int32, (tk, gw), 0)
        qtok = i * tq + (lax.broadcasted_iota(jnp.int32, (tk, gw), 1) & (tq - 1))
        for g, cs in groups:
            put_scores(0, tk, cs, jnp.where(ktok <= qtok, scores(keys(diag0), g), -jnp.inf))

        for t in range(nf + 1):
            cur, nxt = t % 2, 1 - t % 2
            for g, cs in groups:
                if t > 0:
                    add_values(values(block_start(t - 1)), p_ref[nxt, :, cs], cs)
                if t < nf:
                    put_scores(nxt, tk, cs, scores(keys(block_start(t + 1)), g))
                else:
                    put_scores(nxt, N_META, cs, scores(kmeta_ref[0], g))
                softmax(cur, tk, cs)
        last, meta = nf % 2, 1 - nf % 2
        for g, cs in groups:
            add_values(values(block_start(nf)), p_ref[last, :, cs], cs)
            softmax(meta, N_META, cs)
            add_values(kvtmeta_ref[0], p_ref[meta, 0:N_META, cs], cs)

        o_lat = (acc_ref[...] / l_ref[...]).astype(BF16)
        for h in range(MLA_HEADS):
            vs = slice(h * MLA_DV, (h + 1) * MLA_DV)
            o = _dot_tn(o_lat[:, h * tq:(h + 1) * tq], wuv_ref[h])
            o_ref[0, :, vs] = (o * smz_ref[0, :, vs].astype(F32)).astype(BF16)

    for nf in range(max_full + 1):
        pl.when(n_full == nf)(functools.partial(run, nf))


def _mla(qm, km, kvt, kmeta, kvtmeta, wuv, smz, tq, tk):
    B, H, S, _ = qm.shape
    assert tq & (tq - 1) == 0 and S % tk == 0 and tk % tq == 0
    grid = (B, S // tq)
    full = lambda a: pl.BlockSpec(a.shape, lambda b, i: (0,) * a.ndim)
    rows = H * tq
    return pl.pallas_call(
        functools.partial(_mla_kernel, tq=tq, tk=tk, max_full=(S - tq) // tk),
        grid=grid,
        in_specs=[pl.BlockSpec((1, H, tq, MLA_KPAD), lambda b, i: (b, 0, i, 0)),
                  pl.BlockSpec((1, S, MLA_KPAD), lambda b, i: (b, 0, 0)),
                  pl.BlockSpec((1, MLA_VT_ROWS, S), lambda b, i: (b, 0, 0)),
                  full(kmeta), full(kvtmeta), full(wuv),
                  pl.BlockSpec((1, tq, MLA_VW), lambda b, i: (b, i, 0))],
        out_specs=pl.BlockSpec((1, tq, MLA_VW), lambda b, i: (b, i, 0)),
        out_shape=jax.ShapeDtypeStruct((B, S, MLA_VW), BF16),
        scratch_shapes=[pltpu.VMEM((1, rows), F32), pltpu.VMEM((1, rows), F32),
                        pltpu.VMEM((1, rows), F32),
                        pltpu.VMEM((MLA_KV_RANK, rows), F32),
                        pltpu.VMEM((2, tk, rows), F32), pltpu.VMEM((2, tk, rows), BF16),
                        pltpu.VMEM((2, SUBLANES, rows), F32)],
        compiler_params=pltpu.CompilerParams(
            dimension_semantics=("parallel", "arbitrary"), vmem_limit_bytes=VMEM_LIMIT),
        name="mla",
    )(qm, km, kvt, kmeta, kvtmeta, wuv, smz)


def _swap_halves(w):
    half = w.shape[-1] // 2
    return jnp.concatenate([w[..., half:], w[..., :half]], axis=-1)


def _prep_weights(w_in, gla_gate_w, mla_w_uq, mla_w_ukv):
    cuts = [0]
    for s in SPLITS:
        cuts.append(cuts[-1] + s)
    (g_q, g_k, g_v, g_lr, g_z, m_cq, m_ckv, m_kr, m_z, gate_gla, gate_mla) = [
        w_in[:, cuts[n]:cuts[n + 1]] for n in range(len(SPLITS))]
    m_krs = _swap_halves(m_kr)
    pad = jnp.zeros((D_MODEL, LANES - GLA_GATE_RANK), w_in.dtype)
    small = jnp.concatenate([m_cq, m_ckv, m_kr, m_kr, m_krs, m_krs, g_lr, pad], axis=1)
    assert small.shape == (D_MODEL, W_SMALL)
    wide = tuple(w.astype(BF16) for w in (g_q, g_k, g_v, g_z, m_z, gate_gla, gate_mla, small))

    uq = mla_w_uq.reshape(MLA_Q_RANK, MLA_HEADS, MLA_QK)
    uq_nope = uq[:, :, :MLA_NOPE].reshape(MLA_Q_RANK, MLA_HEADS * MLA_NOPE)
    uq_rope = uq[:, :, MLA_NOPE:]
    wuq = jnp.concatenate([uq_nope, uq_rope.reshape(MLA_Q_RANK, -1),
                           _swap_halves(uq_rope).reshape(MLA_Q_RANK, -1)], axis=1).astype(BF16)

    ukv = mla_w_ukv.reshape(MLA_KV_RANK, MLA_HEADS, MLA_NOPE + MLA_DV)
    wukt = jnp.transpose(ukv[:, :, :MLA_NOPE], (1, 2, 0)).astype(BF16)
    zero = jnp.zeros_like(wukt[0::2])
    wukt = jnp.concatenate([jnp.concatenate([wukt[0::2], zero], axis=2),
                            jnp.concatenate([zero, wukt[1::2]], axis=2)], axis=1)
    wuv = jnp.transpose(ukv[:, :, MLA_NOPE:], (1, 0, 2)).astype(BF16)

    gw = jnp.concatenate([gla_gate_w, jnp.zeros((LANES - GLA_GATE_RANK, GLA_KW), gla_gate_w.dtype)],
                         axis=0).astype(BF16)
    return wide, wuq, wukt, wuv, gw


def _rope_tables(n):
    inv = 1.0 / (ROPE_BASE ** (jnp.arange(0, MLA_ROPE, 2, dtype=F32) / MLA_ROPE))
    ang = jnp.arange(n, dtype=F32)[:, None] * inv[None, :]
    cos, sin = jnp.cos(ang), jnp.sin(ang)
    cos = jnp.concatenate([cos, cos, cos, cos], axis=1)
    sin = jnp.concatenate([-sin, sin, -sin, sin], axis=1)
    return cos, sin


def kernel(x, meta_tokens, norm_g, w_in, gla_gate_w, gla_gate_b, gla_norm_g, gla_proj,
           mla_q_norm_g, mla_w_uq, mla_kv_norm_g, mla_w_ukv, mla_proj, w_out, final_norm_g):
    B, S, D = x.shape
    assert D == D_MODEL and norm_g.shape[0] == 1 and meta_tokens.shape == (N_META, D_MODEL)

    wide, wuq, wukt, wuv, gw = _prep_weights(w_in[0], gla_gate_w[0], mla_w_uq[0], mla_w_ukv[0])
    row = lambda a: a.reshape(1, -1).astype(F32)
    ng, qg, kvg = row(norm_g[0]), row(mla_q_norm_g[0]), row(mla_kv_norm_g[0])
    gb, gng, fng = row(gla_gate_b[0]), row(gla_norm_g[0]), row(final_norm_g)
    cos, sin = _rope_tables(N_META + S)

    proj_args = (ng, wide, qg, kvg, wuq, wukt)
    meta = _inproj(meta_tokens[None].astype(F32), *proj_args, cos[:N_META], sin[:N_META],
                   tm=N_META, emit_vt=False)
    _, k_m, v_m, _, _, _, _, glr_m, _, kmeta = meta
    kvtmeta = jnp.concatenate([jnp.swapaxes(kmeta[:, :, 0:MLA_KV_RANK], 1, 2),
                               jnp.ones((1, MLA_VT_ROWS - MLA_KV_RANK, N_META), BF16)], axis=1)
    q, k, v, sz, smz, sgg, sgm, glr, qm, km, kvt = _inproj(
        x, *proj_args, cos[N_META:], sin[N_META:], tm=512, emit_vt=True)

    yb = _mla(qm, km, kvt, kmeta, kvtmeta, wuv, smz, tq=256, tk=256)
    return _tail(q, k, v, glr, sz, k_m, v_m, glr_m, gw, gb, gng, yb, sgg, sgm, x,
                 gla_proj[0].astype(BF16), mla_proj[0].astype(BF16), w_out[0].astype(BF16), fng, tc=512)
```

```python
import functools
import math

import jax
import jax.numpy as jnp
from jax import lax
from jax.experimental import pallas as pl
from jax.experimental.pallas import tpu as pltpu

F32 = jnp.float32
BF16 = jnp.bfloat16

D_MODEL = 1024
N_META = 16
EPS = 1e-6

GLA_HEADS = 4
GLA_DK = 128
GLA_DV = 256
GLA_GATE_RANK = 16
GLA_GATE_NORMALIZER = 16.0
GLA_CHUNK = 64
GLA_KW = GLA_HEADS * GLA_DK
GLA_VW = GLA_HEADS * GLA_DV

MLA_HEADS = 8
MLA_NOPE = 128
MLA_ROPE = 64
MLA_DV = 128
MLA_Q_RANK = 256
MLA_KV_RANK = 128
MLA_QK = MLA_NOPE + MLA_ROPE
MLA_VW = MLA_HEADS * MLA_DV
ROPE_BASE = 10000.0

SPLITS = (GLA_KW, GLA_KW, GLA_VW, GLA_GATE_RANK, GLA_VW,
          MLA_Q_RANK, MLA_KV_RANK, MLA_ROPE, MLA_VW, D_MODEL, D_MODEL)

LANES = 128
SUBLANES = 8
SOFTMAX_SLAB = 32
MXU_TILE = 256
MLA_KPAD = 2 * LANES
MLA_VT_ROWS = MLA_KV_RANK + 16
W_SMALL = MLA_Q_RANK + MLA_KV_RANK + 3 * LANES
VMEM_LIMIT = 56 * 1024 * 1024


def _rms(x, g):
    return x * lax.rsqrt(jnp.mean(x * x, axis=-1, keepdims=True) + EPS) * g


def _dot(a, b):
    return jnp.dot(a, b, preferred_element_type=F32)


def _dot_nt(a, b):
    return lax.dot_general(a, b, (((1,), (1,)), ((), ())), preferred_element_type=F32)


def _dot_tn(a, b):
    return lax.dot_general(a, b, (((0,), (0,)), ((), ())), preferred_element_type=F32)


def _inproj_kernel(x_ref, ng_ref, wq_ref, wk_ref, wv_ref, wgz_ref, wmz_ref, wgg_ref, wgm_ref, wsmall_ref,
                   qg_ref, kvg_ref, wuq_ref, wukt_ref, cos_ref, sin_ref,
                   q_ref, k_ref, v_ref, sz_ref, smz_ref, sgg_ref, sgm_ref, glr_ref, qm_ref, km_ref,
                   *maybe_kvt_ref):
    x = x_ref[0]
    u = (x * ng_ref[...]).astype(BF16)
    inv_rms = lax.rsqrt(jnp.mean(x * x, axis=-1, keepdims=True) + EPS)

    half_inv_rms = 0.5 * inv_rms

    def proj(w_ref):
        return _dot(u, w_ref[...]) * inv_rms

    def silu_proj(w_ref):
        h = _dot(u, w_ref[...]) * half_inv_rms
        return h * jnp.tanh(h) + h

    def sigmoid_proj(w_ref):
        return 0.5 * jnp.tanh(_dot(u, w_ref[...]) * half_inv_rms) + 0.5

    small = proj(wsmall_ref)
    cq = small[:, 0:MLA_Q_RANK]
    ckv = small[:, MLA_Q_RANK:MLA_Q_RANK + MLA_KV_RANK]
    o = MLA_Q_RANK + MLA_KV_RANK
    kr = small[:, o:o + LANES]
    krs = small[:, o + LANES:o + 2 * LANES]
    glr_ref[0] = small[:, o + 2 * LANES:o + 3 * LANES].astype(BF16)

    cos = cos_ref[...]
    sin = sin_ref[...]
    ckvn = _rms(ckv, kvg_ref[...])
    km_ref[0, :, 0:LANES] = ckvn.astype(BF16)
    for kvt_ref in maybe_kvt_ref:
        kvt_ref[0, 0:MLA_KV_RANK, :] = ckvn.T.astype(BF16)
        kvt_ref[0, MLA_KV_RANK:MLA_VT_ROWS, :] = jnp.ones((MLA_VT_ROWS - MLA_KV_RANK, x.shape[0]), BF16)
    km_ref[0, :, LANES:MLA_KPAD] = (kr * cos + krs * sin).astype(BF16)
    cqn = _rms(cq, qg_ref[...]).astype(BF16)

    sz_ref[0] = silu_proj(wgz_ref).astype(BF16)
    qall = _dot(cqn, wuq_ref[...])
    smz_ref[0] = silu_proj(wmz_ref).astype(BF16)

    scale = MLA_QK ** -0.5 * math.log2(math.e)
    nope_w = MLA_HEADS * MLA_NOPE
    rope_w = MLA_HEADS * MLA_ROPE
    lane = lax.broadcasted_iota(jnp.int32, (x.shape[0], LANES), 1)
    for g in range(MLA_HEADS // 2):
        qn = qall[:, 2 * g * MLA_NOPE:(2 * g + 2) * MLA_NOPE].astype(BF16)
        qa = _dot(qn, wukt_ref[g]) * scale
        qm_ref[0, 2 * g, :, 0:LANES] = qa[:, 0:LANES].astype(BF16)
        qm_ref[0, 2 * g + 1, :, 0:LANES] = qa[:, LANES:2 * LANES].astype(BF16)
        a = qall[:, nope_w + g * LANES:nope_w + (g + 1) * LANES]
        b = qall[:, nope_w + rope_w + g * LANES:nope_w + rope_w + (g + 1) * LANES]
        r = (a * cos + b * sin) * scale
        qm_ref[0, 2 * g, :, LANES:MLA_KPAD] = jnp.where(lane < MLA_ROPE, r, 0.0).astype(BF16)
        qm_ref[0, 2 * g + 1, :, LANES:MLA_KPAD] = jnp.where(lane >= MLA_ROPE, r, 0.0).astype(BF16)

    sgg_ref[0] = sigmoid_proj(wgg_ref).astype(BF16)
    sgm_ref[0] = sigmoid_proj(wgm_ref).astype(BF16)
    q_ref[0] = (proj(wq_ref) * (GLA_DK ** -0.5)).astype(BF16)
    v_ref[0] = proj(wv_ref).astype(BF16)
    k_ref[0] = proj(wk_ref).astype(BF16)


def _inproj(x, ng, wide, qg, kvg, wuq, wukt, cos, sin, tm, emit_vt):
    B, S, _ = x.shape
    grid = (B, S // tm)
    tok = lambda w: pl.BlockSpec((1, tm, w), lambda b, i: (b, i, 0))
    full = lambda a: pl.BlockSpec(a.shape, lambda b, i: (0,) * a.ndim)
    out_shapes = [
        jax.ShapeDtypeStruct((B, S, GLA_KW), BF16),
        jax.ShapeDtypeStruct((B, S, GLA_KW), BF16),
        jax.ShapeDtypeStruct((B, S, GLA_VW), BF16),
        jax.ShapeDtypeStruct((B, S, GLA_VW), BF16),
        jax.ShapeDtypeStruct((B, S, MLA_VW), BF16),
        jax.ShapeDtypeStruct((B, S, D_MODEL), BF16),
        jax.ShapeDtypeStruct((B, S, D_MODEL), BF16),
        jax.ShapeDtypeStruct((B, S, LANES), BF16),
        jax.ShapeDtypeStruct((B, MLA_HEADS, S, MLA_KPAD), BF16),
        jax.ShapeDtypeStruct((B, S, MLA_KPAD), BF16),
    ]
    out_specs = [tok(GLA_KW), tok(GLA_KW), tok(GLA_VW), tok(GLA_VW), tok(MLA_VW),
                 tok(D_MODEL), tok(D_MODEL), tok(LANES),
                 pl.BlockSpec((1, MLA_HEADS, tm, MLA_KPAD), lambda b, i: (b, 0, i, 0)),
                 tok(MLA_KPAD)]
    if emit_vt:
        out_shapes.append(jax.ShapeDtypeStruct((B, MLA_VT_ROWS, S), BF16))
        out_specs.append(pl.BlockSpec((1, MLA_VT_ROWS, tm), lambda b, i: (b, 0, i)))
    tab = pl.BlockSpec((tm, LANES), lambda b, i: (i, 0))
    return pl.pallas_call(
        _inproj_kernel,
        grid=grid,
        in_specs=[tok(D_MODEL), full(ng), *[full(w) for w in wide],
                  full(qg), full(kvg), full(wuq), full(wukt), tab, tab],
        out_specs=out_specs,
        out_shape=out_shapes,
        compiler_params=pltpu.CompilerParams(
            dimension_semantics=("parallel", "parallel"), vmem_limit_bytes=VMEM_LIMIT),
        name="inproj",
    )(x, ng, *wide, qg, kvg, wuq, wukt, cos, sin)


def _tri(n):
    row = lax.broadcasted_iota(jnp.int32, (n, n), 0)
    col = lax.broadcasted_iota(jnp.int32, (n, n), 1)
    return col <= row


def _gla_log_gates(glr, gw_ref, gb_ref):
    return jax.nn.log_sigmoid(_dot(glr, gw_ref[...]) + gb_ref[...]) / GLA_GATE_NORMALIZER


def _cumsum_rows(tri, g):
    hi = g.astype(BF16)
    lo = (g - hi.astype(F32)).astype(BF16)
    return _dot(tri, hi) + _dot(tri, lo)


def _tail_kernel(q_ref, k_ref, v_ref, glr_ref, sz_ref, km_ref, vm_ref, glrm_ref, gw_ref, gb_ref, gng_ref,
                 yba_ref, ybb_ref, sgg_ref, sgm_ref, x_ref, gp_ref, mp_ref, wo_ref, fng_ref,
                 o_ref, st_ref, oi_ref, u_ref, ya_ref, yacc_ref, merged_ref, *, n_chunks, tiles_per_row):
    C = GLA_CHUNK
    step = pl.program_id(0)
    n_tiles = pl.num_programs(0) - 1
    tile_in_row = jnp.minimum(step, n_tiles - 1) % tiles_per_row
    prev_in_first_half = jnp.maximum(step - 1, 0) % tiles_per_row < tiles_per_row // 2

    @pl.when(step == 0)
    def _no_previous_tile():
        ya_ref[...] = jnp.zeros(ya_ref.shape, BF16)

    @pl.when(tile_in_row == 0)
    def _init_state():
        b = _cumsum_rows(_tri(N_META).astype(BF16), _gla_log_gates(glrm_ref[0], gw_ref, gb_ref))
        kl = (km_ref[0].astype(F32) * jnp.exp(b[-1:, :] - b)).astype(BF16)
        for h in range(GLA_HEADS):
            st_ref[h] = _dot_tn(vm_ref[0, :, h * GLA_DV:(h + 1) * GLA_DV],
                                kl[:, h * GLA_DK:(h + 1) * GLA_DK])

    causal = _tri(C)
    tri = causal.astype(BF16)
    gng = gng_ref[...]
    heads = [(slice(h * GLA_DK, (h + 1) * GLA_DK), slice(h * GLA_DV, (h + 1) * GLA_DV))
             for h in range(GLA_HEADS)]
    chunks = [slice(c * C, (c + 1) * C) for c in range(n_chunks)]
    col_blocks = [slice(c, c + MXU_TILE) for c in range(0, D_MODEL, MXU_TILE)]

    def merge_block(cs, rs):
        yb = jnp.where(prev_in_first_half, yba_ref[0, rs, :], ybb_ref[0, rs, :])
        y_b = _dot(yb, mp_ref[:, cs])
        merged = (sgg_ref[0, rs, cs].astype(F32) * yacc_ref[rs, cs]
                  + sgm_ref[0, rs, cs].astype(F32) * y_b)
        merged_ref[rs, cs] = merged.astype(BF16)

    def out_block(cs, rs):
        h = x_ref[0, rs, cs] + _dot(merged_ref[rs, :], wo_ref[:, cs])
        o_ref[0, rs, cs] = h
        return jnp.sum(h * h, axis=-1, keepdims=True)

    gate_logits = _dot(glr_ref[0], gw_ref[...]) + gb_ref[...]
    for cs in col_blocks:
        yacc_ref[:, cs] = _dot(ya_ref[...], gp_ref[:, cs])

    g = jax.nn.log_sigmoid(gate_logits) / GLA_GATE_NORMALIZER
    b = jnp.concatenate([_cumsum_rows(tri, g[rows]) for rows in chunks], axis=0)
    b3 = b.reshape(n_chunks, C, GLA_KW)
    b_last = b3[:, C - 1:C, :]
    q = q_ref[0].astype(F32)
    k = k_ref[0].astype(F32)
    qe = (q * jnp.exp(b)).astype(BF16)
    ke = (k * jnp.exp(-b)).astype(BF16)
    kl = (k * jnp.exp(b_last - b3).reshape(n_chunks * C, GLA_KW)).astype(BF16)
    decay = jnp.exp(b_last)

    for cs in col_blocks[:-1]:
        merge_block(cs, slice(None))

    blocks = [(c, rows, h, ks, vs) for c, rows in enumerate(chunks) for h, (ks, vs) in enumerate(heads)]
    a = [jnp.where(causal, _dot_nt(qe[rows, ks], ke[rows, ks]), 0.0).astype(BF16)
         for _, rows, _, ks, _ in blocks]
    for c, rows, h, ks, vs in blocks:
        u_ref[c, h] = _dot_tn(v_ref[0, rows, vs], kl[rows, ks])
    for a_ch, (_, rows, _, _, vs) in zip(a, blocks):
        oi_ref[rows, vs] = _dot(a_ch, v_ref[0, rows, vs])

    def gla_chunk(c):
        rows = chunks[c]
        for h, (ks, vs) in enumerate(heads):
            st = st_ref[h]
            o = oi_ref[rows, vs] + _dot_nt(qe[rows, ks], st.astype(BF16))
            st_ref[h] = st * decay[c, :, ks] + u_ref[c, h]
            o = _rms(o, gng) * sz_ref[0, rows, vs].astype(F32)
            ya_ref[rows, vs] = o.astype(BF16)

    half = n_chunks * C // 2
    halves = [slice(0, half), slice(half, 2 * half)]
    work = [functools.partial(merge_block, col_blocks[-1], rs) for rs in halves]
    work += [functools.partial(out_block, cs, rs) for cs in col_blocks for rs in halves]
    results = [w() for w in work[:len(work) - n_chunks]]
    for c, w in enumerate(work[len(work) - n_chunks:]):
        results.append(w())
        gla_chunk(c)
    sumsq = [functools.reduce(jnp.add, results[2 + r::2]) for r in range(2)]
    for rs, ss in zip(halves, sumsq):
        o_ref[0, rs, :] = o_ref[0, rs, :] * lax.rsqrt(ss / D_MODEL + EPS) * fng_ref[...]


def _tail(q, k, v, glr, sz, km, vm, glrm, gw, gb, gng, yba, ybb, sgg, sgm, x, gp, mp, wo, fng, tc):
    B, S, _ = q.shape
    n = S // tc
    n_tiles = B * n
    gla_tile = lambda s: jnp.minimum(s, n_tiles - 1)
    out_tile = lambda s: jnp.maximum(s - 1, 0)
    gtok = lambda w: pl.BlockSpec((1, tc, w), lambda s: (gla_tile(s) // n, gla_tile(s) % n, 0))
    otok = pl.BlockSpec((1, tc, D_MODEL), lambda s: (out_tile(s) // n, out_tile(s) % n, 0))
    ytok_a = pl.BlockSpec((1, tc, D_MODEL),
                          lambda s: (out_tile(s) // n, jnp.minimum(out_tile(s) % n, n // 2 - 1), 0))
    ytok_b = pl.BlockSpec((1, tc, D_MODEL),
                          lambda s: (out_tile(s) // n, jnp.maximum(out_tile(s) % n - n // 2, 0), 0))
    full = lambda a: pl.BlockSpec(a.shape, lambda s: (0,) * a.ndim)
    n_chunks = tc // GLA_CHUNK
    return pl.pallas_call(
        functools.partial(_tail_kernel, n_chunks=n_chunks, tiles_per_row=n),
        grid=(n_tiles + 1,),
        in_specs=[gtok(GLA_KW), gtok(GLA_KW), gtok(GLA_VW), gtok(LANES), gtok(GLA_VW),
                  full(km), full(vm), full(glrm), full(gw), full(gb), full(gng),
                  ytok_a, ytok_b, otok, otok, otok, full(gp), full(mp), full(wo), full(fng)],
        out_specs=otok,
        out_shape=jax.ShapeDtypeStruct((B, S, D_MODEL), F32),
        scratch_shapes=[pltpu.VMEM((GLA_HEADS, GLA_DV, GLA_DK), F32),
                        pltpu.VMEM((tc, GLA_VW), F32),
                        pltpu.VMEM((n_chunks, GLA_HEADS, GLA_DV, GLA_DK), F32),
                        pltpu.VMEM((tc, GLA_VW), BF16),
                        pltpu.VMEM((tc, D_MODEL), F32),
                        pltpu.VMEM((tc, D_MODEL), BF16)],
        compiler_params=pltpu.CompilerParams(
            dimension_semantics=("arbitrary",), vmem_limit_bytes=VMEM_LIMIT),
        name="tail",
    )(q, k, v, glr, sz, km, vm, glrm, gw, gb, gng, yba, ybb, sgg, sgm, x, gp, mp, wo, fng)


def _mla_kernel(qa_ref, qb_ref, km_ref, kvt_ref, kmeta_ref, kvtmeta_ref, wuv_ref, smza_ref, smzb_ref,
                oa_ref, ob_ref, m_ref, l_ref, alpha_ref, acc_ref, s_ref, p_ref, bmax_ref, *, tq, tk, n_tiles):
    assert tq == tk
    cols = MLA_HEADS * tq
    gw = MLA_KPAD
    hpg = gw // tq
    groups = [(g, slice(g * gw, (g + 1) * gw)) for g in range(cols // gw)]

    def keys(k0):
        return km_ref[0, k0:k0 + tk, :]

    def values(k0):
        return kvt_ref[0, :, k0:k0 + tk]

    def fold(x, op):
        return functools.reduce(op, [x[r:r + SUBLANES] for r in range(0, x.shape[0], SUBLANES)])

    def tile_stages(tile, slot, q_ref, smz_ref, o_ref):
        nf = tile
        diag0 = nf * tk
        block_start = lambda t: diag0 if t == 0 else (t - 1) * tk

        def scores(kb, g):
            return _dot_nt(kb, q_ref[0, g * hpg:(g + 1) * hpg].reshape(gw, MLA_KPAD))

        def add_values(vt, p, cs):
            r = _dot(vt, p)
            alpha = alpha_ref[slot, :, cs]
            acc_ref[slot, :, cs] = alpha * acc_ref[slot, :, cs] + r[0:MLA_KV_RANK]
            l_ref[slot, :, cs] = alpha * l_ref[slot, :, cs] + r[MLA_KV_RANK:MLA_KV_RANK + 1]

        def put_scores(buf, n, cs, s):
            s_ref[slot, buf, 0:n, cs] = s
            bmax_ref[slot, buf, :, cs] = fold(s, jnp.maximum)

        def softmax(buf, n, cs):
            slab = min(n, SOFTMAX_SLAB)
            m_old = m_ref[slot, :, cs]
            m_new = jnp.maximum(m_old, jnp.max(bmax_ref[slot, buf, :, cs], axis=0, keepdims=True))
            alpha = jnp.exp2(m_old - m_new)
            for r in range(0, n, slab):
                p_ref[slot, buf, r:r + slab, cs] = jnp.exp2(
                    s_ref[slot, buf, r:r + slab, cs] - m_new).astype(BF16)
            m_ref[slot, :, cs] = m_new
            alpha_ref[slot, :, cs] = alpha

        def start():
            m_ref[slot] = jnp.full(m_ref.shape[1:], -jnp.inf, F32)
            l_ref[slot] = jnp.zeros(l_ref.shape[1:], F32)
            alpha_ref[slot] = jnp.zeros(alpha_ref.shape[1:], F32)
            acc_ref[slot] = jnp.zeros(acc_ref.shape[1:], F32)

        def diagonal(g, cs):
            ktok = diag0 + lax.broadcasted_iota(jnp.int32, (tk, gw), 0)
            qtok = tile * tq + (lax.broadcasted_iota(jnp.int32, (tk, gw), 1) & (tq - 1))
            put_scores(0, tk, cs, jnp.where(ktok <= qtok, scores(keys(diag0), g), -jnp.inf))

        def block(t, g, cs):
            cur, nxt = t % 2, 1 - t % 2
            if t > 0:
                add_values(values(block_start(t - 1)), p_ref[slot, nxt, :, cs], cs)
            if t < nf:
                put_scores(nxt, tk, cs, scores(keys(block_start(t + 1)), g))
            else:
                put_scores(nxt, N_META, cs, scores(kmeta_ref[0], g))
            softmax(cur, tk, cs)

        def finish(g, cs):
            last, meta = nf % 2, 1 - nf % 2
            add_values(values(block_start(nf)), p_ref[slot, last, :, cs], cs)
            softmax(meta, N_META, cs)
            add_values(kvtmeta_ref[0], p_ref[slot, meta, 0:N_META, cs], cs)

        def head_out(h):
            hs = slice(h * tq, (h + 1) * tq)
            vs = slice(h * MLA_DV, (h + 1) * MLA_DV)
            o_lat = (acc_ref[slot, :, hs] / l_ref[slot, :, hs]).astype(BF16)
            o = _dot_tn(o_lat, wuv_ref[h])
            o_ref[0, :, vs] = (o * smz_ref[0, :, vs].astype(F32)).astype(BF16)

        stages = [start]
        stages += [functools.partial(diagonal, g, cs) for g, cs in groups]
        stages += [functools.partial(block, t, g, cs) for t in range(nf + 1) for g, cs in groups]
        stages += [functools.partial(finish, g, cs) for g, cs in groups]
        stages += [functools.partial(head_out, h) for h in range(MLA_HEADS)]
        return stages

    def run_pair(j):
        a = tile_stages(j, 0, qa_ref, smza_ref, oa_ref)
        b = tile_stages(n_tiles - 1 - j, 1, qb_ref, smzb_ref, ob_ref)
        ia = ib = 0
        while ia < len(a) or ib < len(b):
            if ib >= len(b) or (ia < len(a) and (ia + 1) * len(b) <= (ib + 1) * len(a)):
                a[ia](); ia += 1
            else:
                b[ib](); ib += 1

    for j in range(n_tiles // 2):
        pl.when(pl.program_id(1) == j)(functools.partial(run_pair, j))


def _mla(qm, km, kvt, kmeta, kvtmeta, wuv, smz, tq, tk):
    B, H, S, _ = qm.shape
    nq = S // tq
    assert nq % 2 == 0 and S % tk == 0
    grid = (B, nq // 2)
    full = lambda a: pl.BlockSpec(a.shape, lambda b, j: (0,) * a.ndim)
    cols = H * tq
    half = jax.ShapeDtypeStruct((B, S // 2, MLA_VW), BF16)
    return pl.pallas_call(
        functools.partial(_mla_kernel, tq=tq, tk=tk, n_tiles=nq),
        grid=grid,
        in_specs=[pl.BlockSpec((1, H, tq, MLA_KPAD), lambda b, j: (b, 0, j, 0)),
                  pl.BlockSpec((1, H, tq, MLA_KPAD), lambda b, j: (b, 0, nq - 1 - j, 0)),
                  pl.BlockSpec((1, S, MLA_KPAD), lambda b, j: (b, 0, 0)),
                  pl.BlockSpec((1, MLA_VT_ROWS, S), lambda b, j: (b, 0, 0)),
                  full(kmeta), full(kvtmeta), full(wuv),
                  pl.BlockSpec((1, tq, MLA_VW), lambda b, j: (b, j, 0)),
                  pl.BlockSpec((1, tq, MLA_VW), lambda b, j: (b, nq - 1 - j, 0))],
        out_specs=[pl.BlockSpec((1, tq, MLA_VW), lambda b, j: (b, j, 0)),
                   pl.BlockSpec((1, tq, MLA_VW), lambda b, j: (b, nq // 2 - 1 - j, 0))],
        out_shape=[half, half],
        scratch_shapes=[pltpu.VMEM((2, 1, cols), F32), pltpu.VMEM((2, 1, cols), F32),
                        pltpu.VMEM((2, 1, cols), F32),
                        pltpu.VMEM((2, MLA_KV_RANK, cols), F32),
                        pltpu.VMEM((2, 2, tk, cols), F32), pltpu.VMEM((2, 2, tk, cols), BF16),
                        pltpu.VMEM((2, 2, SUBLANES, cols), F32)],
        compiler_params=pltpu.CompilerParams(
            dimension_semantics=("parallel", "arbitrary"), vmem_limit_bytes=VMEM_LIMIT),
        name="mla",
    )(qm, qm, km, kvt, kmeta, kvtmeta, wuv, smz, smz)


def _swap_halves(w):
    half = w.shape[-1] // 2
    return jnp.concatenate([w[..., half:], w[..., :half]], axis=-1)


def _prep_weights(w_in, gla_gate_w, mla_w_uq, mla_w_ukv):
    cuts = [0]
    for s in SPLITS:
        cuts.append(cuts[-1] + s)
    (g_q, g_k, g_v, g_lr, g_z, m_cq, m_ckv, m_kr, m_z, gate_gla, gate_mla) = [
        w_in[:, cuts[n]:cuts[n + 1]] for n in range(len(SPLITS))]
    m_krs = _swap_halves(m_kr)
    pad = jnp.zeros((D_MODEL, LANES - GLA_GATE_RANK), w_in.dtype)
    small = jnp.concatenate([m_cq, m_ckv, m_kr, m_kr, m_krs, m_krs, g_lr, pad], axis=1)
    assert small.shape == (D_MODEL, W_SMALL)
    wide = tuple(w.astype(BF16) for w in (g_q, g_k, g_v, g_z, m_z, gate_gla, gate_mla, small))

    uq = mla_w_uq.reshape(MLA_Q_RANK, MLA_HEADS, MLA_QK)
    uq_nope = uq[:, :, :MLA_NOPE].reshape(MLA_Q_RANK, MLA_HEADS * MLA_NOPE)
    uq_rope = uq[:, :, MLA_NOPE:]
    wuq = jnp.concatenate([uq_nope, uq_rope.reshape(MLA_Q_RANK, -1),
                           _swap_halves(uq_rope).reshape(MLA_Q_RANK, -1)], axis=1).astype(BF16)

    ukv = mla_w_ukv.reshape(MLA_KV_RANK, MLA_HEADS, MLA_NOPE + MLA_DV)
    wukt = jnp.transpose(ukv[:, :, :MLA_NOPE], (1, 2, 0)).astype(BF16)
    zero = jnp.zeros_like(wukt[0::2])
    wukt = jnp.concatenate([jnp.concatenate([wukt[0::2], zero], axis=2),
                            jnp.concatenate([zero, wukt[1::2]], axis=2)], axis=1)
    wuv = jnp.transpose(ukv[:, :, MLA_NOPE:], (1, 0, 2)).astype(BF16)

    gw = jnp.concatenate([gla_gate_w, jnp.zeros((LANES - GLA_GATE_RANK, GLA_KW), gla_gate_w.dtype)],
                         axis=0).astype(BF16)
    return wide, wuq, wukt, wuv, gw


def _rope_tables(n):
    inv = 1.0 / (ROPE_BASE ** (jnp.arange(0, MLA_ROPE, 2, dtype=F32) / MLA_ROPE))
    ang = jnp.arange(n, dtype=F32)[:, None] * inv[None, :]
    cos, sin = jnp.cos(ang), jnp.sin(ang)
    cos = jnp.concatenate([cos, cos, cos, cos], axis=1)
    sin = jnp.concatenate([-sin, sin, -sin, sin], axis=1)
    return cos, sin


def kernel(x, meta_tokens, norm_g, w_in, gla_gate_w, gla_gate_b, gla_norm_g, gla_proj,
           mla_q_norm_g, mla_w_uq, mla_kv_norm_g, mla_w_ukv, mla_proj, w_out, final_norm_g):
    B, S, D = x.shape
    assert D == D_MODEL and norm_g.shape[0] == 1 and meta_tokens.shape == (N_META, D_MODEL)

    wide, wuq, wukt, wuv, gw = _prep_weights(w_in[0], gla_gate_w[0], mla_w_uq[0], mla_w_ukv[0])
    row = lambda a: a.reshape(1, -1).astype(F32)
    ng, qg, kvg = row(norm_g[0]), row(mla_q_norm_g[0]), row(mla_kv_norm_g[0])
    gb, gng, fng = row(gla_gate_b[0]), row(gla_norm_g[0]), row(final_norm_g)
    cos, sin = _rope_tables(N_META + S)

    proj_args = (ng, wide, qg, kvg, wuq, wukt)
    meta = _inproj(meta_tokens[None].astype(F32), *proj_args, cos[:N_META], sin[:N_META],
                   tm=N_META, emit_vt=False)
    _, k_m, v_m, _, _, _, _, glr_m, _, kmeta = meta
    kvtmeta = jnp.concatenate([jnp.swapaxes(kmeta[:, :, 0:MLA_KV_RANK], 1, 2),
                               jnp.ones((1, MLA_VT_ROWS - MLA_KV_RANK, N_META), BF16)], axis=1)
    q, k, v, sz, smz, sgg, sgm, glr, qm, km, kvt = _inproj(
        x, *proj_args, cos[N_META:], sin[N_META:], tm=512, emit_vt=True)

    yba, ybb = _mla(qm, km, kvt, kmeta, kvtmeta, wuv, smz, tq=256, tk=256)
    return _tail(q, k, v, glr, sz, k_m, v_m, glr_m, gw, gb, gng, yba, ybb, sgg, sgm, x,
                 gla_proj[0].astype(BF16), mla_proj[0].astype(BF16), w_out[0].astype(BF16), fng, tc=512)
```

```python
import functools
import math

import jax
import jax.numpy as jnp
from jax import lax
from jax.experimental import pallas as pl
from jax.experimental.pallas import tpu as pltpu

F32 = jnp.float32
BF16 = jnp.bfloat16

D_MODEL = 1024
N_META = 16
EPS = 1e-6

GLA_HEADS = 4
GLA_DK = 128
GLA_DV = 256
GLA_GATE_RANK = 16
GLA_GATE_NORMALIZER = 16.0
GLA_CHUNK = 64
GLA_KW = GLA_HEADS * GLA_DK
GLA_VW = GLA_HEADS * GLA_DV

MLA_HEADS = 8
MLA_NOPE = 128
MLA_ROPE = 64
MLA_DV = 128
MLA_Q_RANK = 256
MLA_KV_RANK = 128
MLA_QK = MLA_NOPE + MLA_ROPE
MLA_VW = MLA_HEADS * MLA_DV
ROPE_BASE = 10000.0

SPLITS = (GLA_KW, GLA_KW, GLA_VW, GLA_GATE_RANK, GLA_VW,
          MLA_Q_RANK, MLA_KV_RANK, MLA_ROPE, MLA_VW, D_MODEL, D_MODEL)

LANES = 128
SUBLANES = 8
SOFTMAX_SLAB = 32
MXU_TILE = 256
MLA_KPAD = 2 * LANES
MLA_VT_ROWS = MLA_KV_RANK + 16
W_SMALL = MLA_Q_RANK + MLA_KV_RANK + 3 * LANES
VMEM_LIMIT = 56 * 1024 * 1024


def _rms(x, g):
    return x * lax.rsqrt(jnp.mean(x * x, axis=-1, keepdims=True) + EPS) * g


def _dot(a, b):
    return jnp.dot(a, b, preferred_element_type=F32)


def _dot_nt(a, b):
    return lax.dot_general(a, b, (((1,), (1,)), ((), ())), preferred_element_type=F32)


def _dot_tn(a, b):
    return lax.dot_general(a, b, (((0,), (0,)), ((), ())), preferred_element_type=F32)


def _inproj_kernel(x_ref, ng_ref, wq_ref, wk_ref, wv_ref, wgz_ref, wmz_ref, wgg_ref, wgm_ref, wsmall_ref,
                   qg_ref, kvg_ref, wuq_ref, wukt_ref, cos_ref, sin_ref,
                   q_ref, k_ref, v_ref, sz_ref, smz_ref, sgg_ref, sgm_ref, glr_ref, qm_ref, km_ref,
                   *maybe_kvt_ref):
    x = x_ref[0]
    u = (x * ng_ref[...]).astype(BF16)
    inv_rms = lax.rsqrt(jnp.mean(x * x, axis=-1, keepdims=True) + EPS)

    half_inv_rms = 0.5 * inv_rms

    def proj(w_ref):
        return _dot(u, w_ref[...]) * inv_rms

    def silu_proj(w_ref):
        h = _dot(u, w_ref[...]) * half_inv_rms
        return h * jnp.tanh(h) + h

    def sigmoid_proj(w_ref):
        return 0.5 * jnp.tanh(_dot(u, w_ref[...]) * half_inv_rms) + 0.5

    small = proj(wsmall_ref)
    cq = small[:, 0:MLA_Q_RANK]
    ckv = small[:, MLA_Q_RANK:MLA_Q_RANK + MLA_KV_RANK]
    o = MLA_Q_RANK + MLA_KV_RANK
    kr = small[:, o:o + LANES]
    krs = small[:, o + LANES:o + 2 * LANES]
    glr_ref[0] = small[:, o + 2 * LANES:o + 3 * LANES].astype(BF16)

    cos = cos_ref[...]
    sin = sin_ref[...]
    ckvn = _rms(ckv, kvg_ref[...])
    km_ref[0, :, 0:LANES] = ckvn.astype(BF16)
    for kvt_ref in maybe_kvt_ref:
        kvt_ref[0, 0:MLA_KV_RANK, :] = ckvn.T.astype(BF16)
        kvt_ref[0, MLA_KV_RANK:MLA_VT_ROWS, :] = jnp.ones((MLA_VT_ROWS - MLA_KV_RANK, x.shape[0]), BF16)
    km_ref[0, :, LANES:MLA_KPAD] = (kr * cos + krs * sin).astype(BF16)
    cqn = _rms(cq, qg_ref[...]).astype(BF16)

    sz_ref[0] = silu_proj(wgz_ref).astype(BF16)
    qall = _dot(cqn, wuq_ref[...])
    smz_ref[0] = silu_proj(wmz_ref).astype(BF16)

    scale = MLA_QK ** -0.5 * math.log2(math.e)
    nope_w = MLA_HEADS * MLA_NOPE
    rope_w = MLA_HEADS * MLA_ROPE
    lane = lax.broadcasted_iota(jnp.int32, (x.shape[0], LANES), 1)
    for g in range(MLA_HEADS // 2):
        qn = qall[:, 2 * g * MLA_NOPE:(2 * g + 2) * MLA_NOPE].astype(BF16)
        qa = _dot(qn, wukt_ref[g]) * scale
        qm_ref[0, 2 * g, :, 0:LANES] = qa[:, 0:LANES].astype(BF16)
        qm_ref[0, 2 * g + 1, :, 0:LANES] = qa[:, LANES:2 * LANES].astype(BF16)
        a = qall[:, nope_w + g * LANES:nope_w + (g + 1) * LANES]
        b = qall[:, nope_w + rope_w + g * LANES:nope_w + rope_w + (g + 1) * LANES]
        r = (a * cos + b * sin) * scale
        qm_ref[0, 2 * g, :, LANES:MLA_KPAD] = jnp.where(lane < MLA_ROPE, r, 0.0).astype(BF16)
        qm_ref[0, 2 * g + 1, :, LANES:MLA_KPAD] = jnp.where(lane >= MLA_ROPE, r, 0.0).astype(BF16)

    sgg_ref[0] = sigmoid_proj(wgg_ref).astype(BF16)
    sgm_ref[0] = sigmoid_proj(wgm_ref).astype(BF16)
    q_ref[0] = (proj(wq_ref) * (GLA_DK ** -0.5)).astype(BF16)
    v_ref[0] = proj(wv_ref).astype(BF16)
    k_ref[0] = proj(wk_ref).astype(BF16)


def _inproj(x, ng, wide, qg, kvg, wuq, wukt, cos, sin, tm, emit_vt):
    B, S, _ = x.shape
    grid = (B, S // tm)
    tok = lambda w: pl.BlockSpec((1, tm, w), lambda b, i: (b, i, 0))
    full = lambda a: pl.BlockSpec(a.shape, lambda b, i: (0,) * a.ndim)
    out_shapes = [
        jax.ShapeDtypeStruct((B, S, GLA_KW), BF16),
        jax.ShapeDtypeStruct((B, S, GLA_KW), BF16),
        jax.ShapeDtypeStruct((B, S, GLA_VW), BF16),
        jax.ShapeDtypeStruct((B, S, GLA_VW), BF16),
        jax.ShapeDtypeStruct((B, S, MLA_VW), BF16),
        jax.ShapeDtypeStruct((B, S, D_MODEL), BF16),
        jax.ShapeDtypeStruct((B, S, D_MODEL), BF16),
        jax.ShapeDtypeStruct((B, S, LANES), BF16),
        jax.ShapeDtypeStruct((B, MLA_HEADS, S, MLA_KPAD), BF16),
        jax.ShapeDtypeStruct((B, S, MLA_KPAD), BF16),
    ]
    out_specs = [tok(GLA_KW), tok(GLA_KW), tok(GLA_VW), tok(GLA_VW), tok(MLA_VW),
                 tok(D_MODEL), tok(D_MODEL), tok(LANES),
                 pl.BlockSpec((1, MLA_HEADS, tm, MLA_KPAD), lambda b, i: (b, 0, i, 0)),
                 tok(MLA_KPAD)]
    if emit_vt:
        out_shapes.append(jax.ShapeDtypeStruct((B, MLA_VT_ROWS, S), BF16))
        out_specs.append(pl.BlockSpec((1, MLA_VT_ROWS, tm), lambda b, i: (b, 0, i)))
    tab = pl.BlockSpec((tm, LANES), lambda b, i: (i, 0))
    return pl.pallas_call(
        _inproj_kernel,
        grid=grid,
        in_specs=[tok(D_MODEL), full(ng), *[full(w) for w in wide],
                  full(qg), full(kvg), full(wuq), full(wukt), tab, tab],
        out_specs=out_specs,
        out_shape=out_shapes,
        compiler_params=pltpu.CompilerParams(
            dimension_semantics=("parallel", "parallel"), vmem_limit_bytes=VMEM_LIMIT),
        name="inproj",
    )(x, ng, *wide, qg, kvg, wuq, wukt, cos, sin)


def _tri(n):
    row = lax.broadcasted_iota(jnp.int32, (n, n), 0)
    col = lax.broadcasted_iota(jnp.int32, (n, n), 1)
    return col <= row


def _log_sigmoid(z):
    return jnp.minimum(z, 0.0) - jnp.log1p(jnp.exp(-jnp.abs(z)))


def _gla_log_gates(glr, gw_ref, gb_ref):
    return _log_sigmoid(_dot(glr, gw_ref[...]) + gb_ref[...]) / GLA_GATE_NORMALIZER


def _cumsum_rows(tri, g):
    hi = g.astype(BF16)
    lo = (g - hi.astype(F32)).astype(BF16)
    return _dot(tri, hi) + _dot(tri, lo)


def _tail_kernel(q_ref, k_ref, v_ref, glr_ref, sz_ref, km_ref, vm_ref, glrm_ref, gw_ref, gb_ref, gng_ref,
                 yba_ref, ybb_ref, sgg_ref, sgm_ref, x_ref, gp_ref, mp_ref, wo_ref, fng_ref,
                 o_ref, st_ref, oi_ref, u_ref, ya_ref, yacc_ref, merged_ref, *, n_chunks, tiles_per_row):
    C = GLA_CHUNK
    step = pl.program_id(0)
    n_tiles = pl.num_programs(0) - 1
    tile_in_row = jnp.minimum(step, n_tiles - 1) % tiles_per_row
    prev_in_first_half = jnp.maximum(step - 1, 0) % tiles_per_row < tiles_per_row // 2

    @pl.when(step == 0)
    def _no_previous_tile():
        ya_ref[...] = jnp.zeros(ya_ref.shape, BF16)

    @pl.when(tile_in_row == 0)
    def _init_state():
        b = _cumsum_rows(_tri(N_META).astype(BF16), _gla_log_gates(glrm_ref[0], gw_ref, gb_ref))
        kl = (km_ref[0].astype(F32) * jnp.exp(b[-1:, :] - b)).astype(BF16)
        for h in range(GLA_HEADS):
            st_ref[h] = _dot_tn(vm_ref[0, :, h * GLA_DV:(h + 1) * GLA_DV],
                                kl[:, h * GLA_DK:(h + 1) * GLA_DK])

    causal = _tri(C)
    tri = causal.astype(BF16)
    gng = gng_ref[...]
    heads = [(slice(h * GLA_DK, (h + 1) * GLA_DK), slice(h * GLA_DV, (h + 1) * GLA_DV))
             for h in range(GLA_HEADS)]
    chunks = [slice(c * C, (c + 1) * C) for c in range(n_chunks)]
    col_blocks = [slice(c, c + MXU_TILE) for c in range(0, D_MODEL, MXU_TILE)]

    def merge_block(cs, rs):
        yb = jnp.where(prev_in_first_half, yba_ref[0, rs, :], ybb_ref[0, rs, :])
        y_b = _dot(yb, mp_ref[:, cs])
        merged = (sgg_ref[0, rs, cs].astype(F32) * yacc_ref[rs, cs]
                  + sgm_ref[0, rs, cs].astype(F32) * y_b)
        merged_ref[rs, cs] = merged.astype(BF16)

    def out_block(cs, rs):
        h = x_ref[0, rs, cs] + _dot(merged_ref[rs, :], wo_ref[:, cs])
        o_ref[0, rs, cs] = h
        return jnp.sum(h * h, axis=-1, keepdims=True)

    gate_logits = _dot(glr_ref[0], gw_ref[...]) + gb_ref[...]
    for cs in col_blocks:
        yacc_ref[:, cs] = _dot(ya_ref[...], gp_ref[:, cs])

    g = _log_sigmoid(gate_logits) / GLA_GATE_NORMALIZER
    b = jnp.concatenate([_cumsum_rows(tri, g[rows]) for rows in chunks], axis=0)
    b3 = b.reshape(n_chunks, C, GLA_KW)
    b_last = b3[:, C - 1:C, :]
    q = q_ref[0].astype(F32)
    k = k_ref[0].astype(F32)
    qe = (q * jnp.exp(b)).astype(BF16)
    ke = (k * jnp.exp(-b)).astype(BF16)
    kl = (k * jnp.exp(b_last - b3).reshape(n_chunks * C, GLA_KW)).astype(BF16)
    decay = jnp.exp(b_last)

    for cs in col_blocks[:-1]:
        merge_block(cs, slice(None))

    blocks = [(c, rows, h, ks, vs) for c, rows in enumerate(chunks) for h, (ks, vs) in enumerate(heads)]
    a = [jnp.where(causal, _dot_nt(qe[rows, ks], ke[rows, ks]), 0.0).astype(BF16)
         for _, rows, _, ks, _ in blocks]
    for c, rows, h, ks, vs in blocks:
        u_ref[c, h] = _dot_tn(v_ref[0, rows, vs], kl[rows, ks])
    for a_ch, (_, rows, _, _, vs) in zip(a, blocks):
        oi_ref[rows, vs] = _dot(a_ch, v_ref[0, rows, vs])

    def gla_chunk(c):
        rows = chunks[c]
        for h, (ks, vs) in enumerate(heads):
            st = st_ref[h]
            o = oi_ref[rows, vs] + _dot_nt(qe[rows, ks], st.astype(BF16))
            st_ref[h] = st * decay[c, :, ks] + u_ref[c, h]
            o = _rms(o, gng) * sz_ref[0, rows, vs].astype(F32)
            ya_ref[rows, vs] = o.astype(BF16)

    half = n_chunks * C // 2
    halves = [slice(0, half), slice(half, 2 * half)]
    def final_norm(rs, sumsq):
        o_ref[0, rs, :] = o_ref[0, rs, :] * lax.rsqrt(sumsq / D_MODEL + EPS) * fng_ref[...]

    work = [functools.partial(merge_block, col_blocks[-1], rs) for rs in halves]
    work += [functools.partial(out_block, cs, rs) for rs in halves for cs in col_blocks]
    n_lead = len(work) - n_chunks
    assert n_lead >= len(halves)
    for w in work[:len(halves)]:
        w()
    sums = []
    for n, w in enumerate(work[len(halves):]):
        sums.append(w())
        if len(sums) == len(col_blocks):
            final_norm(halves[n // len(col_blocks)], functools.reduce(jnp.add, sums))
            sums = []
        if n + len(halves) >= n_lead:
            gla_chunk(n + len(halves) - n_lead)


def _tail(q, k, v, glr, sz, km, vm, glrm, gw, gb, gng, yba, ybb, sgg, sgm, x, gp, mp, wo, fng, tc):
    B, S, _ = q.shape
    n = S // tc
    n_tiles = B * n
    gla_tile = lambda s: jnp.minimum(s, n_tiles - 1)
    out_tile = lambda s: jnp.maximum(s - 1, 0)
    gtok = lambda w: pl.BlockSpec((1, tc, w), lambda s: (gla_tile(s) // n, gla_tile(s) % n, 0))
    otok = pl.BlockSpec((1, tc, D_MODEL), lambda s: (out_tile(s) // n, out_tile(s) % n, 0))
    ytok_a = pl.BlockSpec((1, tc, D_MODEL),
                          lambda s: (out_tile(s) // n, jnp.minimum(out_tile(s) % n, n // 2 - 1), 0))
    ytok_b = pl.BlockSpec((1, tc, D_MODEL),
                          lambda s: (out_tile(s) // n, jnp.maximum(out_tile(s) % n - n // 2, 0), 0))
    full = lambda a: pl.BlockSpec(a.shape, lambda s: (0,) * a.ndim)
    n_chunks = tc // GLA_CHUNK
    return pl.pallas_call(
        functools.partial(_tail_kernel, n_chunks=n_chunks, tiles_per_row=n),
        grid=(n_tiles + 1,),
        in_specs=[gtok(GLA_KW), gtok(GLA_KW), gtok(GLA_VW), gtok(LANES), gtok(GLA_VW),
                  full(km), full(vm), full(glrm), full(gw), full(gb), full(gng),
                  ytok_a, ytok_b, otok, otok, otok, full(gp), full(mp), full(wo), full(fng)],
        out_specs=otok,
        out_shape=jax.ShapeDtypeStruct((B, S, D_MODEL), F32),
        scratch_shapes=[pltpu.VMEM((GLA_HEADS, GLA_DV, GLA_DK), F32),
                        pltpu.VMEM((tc, GLA_VW), F32),
                        pltpu.VMEM((n_chunks, GLA_HEADS, GLA_DV, GLA_DK), F32),
                        pltpu.VMEM((tc, GLA_VW), BF16),
                        pltpu.VMEM((tc, D_MODEL), F32),
                        pltpu.VMEM((tc, D_MODEL), BF16)],
        compiler_params=pltpu.CompilerParams(
            dimension_semantics=("arbitrary",), vmem_limit_bytes=VMEM_LIMIT),
        name="tail",
    )(q, k, v, glr, sz, km, vm, glrm, gw, gb, gng, yba, ybb, sgg, sgm, x, gp, mp, wo, fng)


def _mla_kernel(qa_ref, qb_ref, km_ref, kvt_ref, kmeta_ref, kvtmeta_ref, wuv_ref, smza_ref, smzb_ref,
                oa_ref, ob_ref, m_ref, l_ref, alpha_ref, acc_ref, s_ref, p_ref, bmax_ref, *, tq, tk, n_tiles):
    assert tq == tk
    cols = MLA_HEADS * tq
    gw = MLA_KPAD
    hpg = gw // tq
    groups = [(g, slice(g * gw, (g + 1) * gw)) for g in range(cols // gw)]
    META_BUF = 2

    def keys(k0):
        return km_ref[0, k0:k0 + tk, :]

    def values(k0):
        return kvt_ref[0, :, k0:k0 + tk]

    def fold(x, op):
        return functools.reduce(op, [x[r:r + SUBLANES] for r in range(0, x.shape[0], SUBLANES)])

    def tile_stages(tile, slot, q_ref, smz_ref, o_ref):
        nf = tile
        diag0 = nf * tk
        block_start = lambda t: diag0 if t == 0 else (t - 1) * tk

        def scores(kb, g):
            return _dot_nt(kb, q_ref[0, g * hpg:(g + 1) * hpg].reshape(gw, MLA_KPAD))

        def add_values(vt, p, cs):
            r = _dot(vt, p)
            alpha = alpha_ref[slot, :, cs]
            acc_ref[slot, :, cs] = alpha * acc_ref[slot, :, cs] + r[0:MLA_KV_RANK]
            l_ref[slot, :, cs] = alpha * l_ref[slot, :, cs] + r[MLA_KV_RANK:MLA_KV_RANK + 1]

        def put_scores(buf, n, cs, s):
            s_ref[slot, buf, 0:n, cs] = s
            bmax_ref[slot, buf, :, cs] = fold(s, jnp.maximum)

        def softmax(buf, n, cs):
            slab = min(n, SOFTMAX_SLAB)
            m_old = m_ref[slot, :, cs]
            m_new = jnp.maximum(m_old, jnp.max(bmax_ref[slot, buf, :, cs], axis=0, keepdims=True))
            alpha = jnp.exp2(m_old - m_new)
            for r in range(0, n, slab):
                p_ref[slot, buf, r:r + slab, cs] = jnp.exp2(
                    s_ref[slot, buf, r:r + slab, cs] - m_new).astype(BF16)
            m_ref[slot, :, cs] = m_new
            alpha_ref[slot, :, cs] = alpha

        def start():
            m_ref[slot] = jnp.full(m_ref.shape[1:], -jnp.inf, F32)
            l_ref[slot] = jnp.zeros(l_ref.shape[1:], F32)
            alpha_ref[slot] = jnp.zeros(alpha_ref.shape[1:], F32)
            acc_ref[slot] = jnp.zeros(acc_ref.shape[1:], F32)

        def diagonal(g, cs):
            ktok = diag0 + lax.broadcasted_iota(jnp.int32, (tk, gw), 0)
            qtok = tile * tq + (lax.broadcasted_iota(jnp.int32, (tk, gw), 1) & (tq - 1))
            s = scores(jnp.concatenate([keys(diag0), kmeta_ref[0]], axis=0), g)
            put_scores(0, tk, cs, jnp.where(ktok <= qtok, s[0:tk], -jnp.inf))
            put_scores(META_BUF, N_META, cs, s[tk:tk + N_META])

        def block(t, g, cs):
            cur, nxt = t % 2, 1 - t % 2
            if t > 0:
                add_values(values(block_start(t - 1)), p_ref[slot, nxt, :, cs], cs)
            if t < nf:
                put_scores(nxt, tk, cs, scores(keys(block_start(t + 1)), g))
            softmax(cur, tk, cs)

        def finish(g, cs):
            add_values(values(block_start(nf)), p_ref[slot, nf % 2, :, cs], cs)
            softmax(META_BUF, N_META, cs)
            add_values(kvtmeta_ref[0], p_ref[slot, META_BUF, 0:N_META, cs], cs)

        def head_out(h):
            hs = slice(h * tq, (h + 1) * tq)
            vs = slice(h * MLA_DV, (h + 1) * MLA_DV)
            o_lat = (acc_ref[slot, :, hs] / l_ref[slot, :, hs]).astype(BF16)
            o = _dot_tn(o_lat, wuv_ref[h])
            o_ref[0, :, vs] = (o * smz_ref[0, :, vs].astype(F32)).astype(BF16)

        stages = [start]
        stages += [functools.partial(diagonal, g, cs) for g, cs in groups]
        stages += [functools.partial(block, t, g, cs) for t in range(nf + 1) for g, cs in groups]
        stages += [functools.partial(finish, g, cs) for g, cs in groups]
        stages += [functools.partial(head_out, h) for h in range(MLA_HEADS)]
        return stages

    def run_pair(j):
        a = tile_stages(j, 0, qa_ref, smza_ref, oa_ref)
        b = tile_stages(n_tiles - 1 - j, 1, qb_ref, smzb_ref, ob_ref)
        ia = ib = 0
        while ia < len(a) or ib < len(b):
            if ib >= len(b) or (ia < len(a) and (ia + 1) * len(b) <= (ib + 1) * len(a)):
                a[ia](); ia += 1
            else:
                b[ib](); ib += 1

    for j in range(n_tiles // 2):
        pl.when(pl.program_id(1) == j)(functools.partial(run_pair, j))


def _mla(qm, km, kvt, kmeta, kvtmeta, wuv, smz, tq, tk):
    B, H, S, _ = qm.shape
    nq = S // tq
    assert nq % 2 == 0 and S % tk == 0
    grid = (B, nq // 2)
    full = lambda a: pl.BlockSpec(a.shape, lambda b, j: (0,) * a.ndim)
    cols = H * tq
    half = jax.ShapeDtypeStruct((B, S // 2, MLA_VW), BF16)
    return pl.pallas_call(
        functools.partial(_mla_kernel, tq=tq, tk=tk, n_tiles=nq),
        grid=grid,
        in_specs=[pl.BlockSpec((1, H, tq, MLA_KPAD), lambda b, j: (b, 0, j, 0)),
                  pl.BlockSpec((1, H, tq, MLA_KPAD), lambda b, j: (b, 0, nq - 1 - j, 0)),
                  pl.BlockSpec((1, S, MLA_KPAD), lambda b, j: (b, 0, 0)),
                  pl.BlockSpec((1, MLA_VT_ROWS, S), lambda b, j: (b, 0, 0)),
                  full(kmeta), full(kvtmeta), full(wuv),
                  pl.BlockSpec((1, tq, MLA_VW), lambda b, j: (b, j, 0)),
                  pl.BlockSpec((1, tq, MLA_VW), lambda b, j: (b, nq - 1 - j, 0))],
        out_specs=[pl.BlockSpec((1, tq, MLA_VW), lambda b, j: (b, j, 0)),
                   pl.BlockSpec((1, tq, MLA_VW), lambda b, j: (b, nq // 2 - 1 - j, 0))],
        out_shape=[half, half],
        scratch_shapes=[pltpu.VMEM((2, 1, cols), F32), pltpu.VMEM((2, 1, cols), F32),
                        pltpu.VMEM((2, 1, cols), F32),
                        pltpu.VMEM((2, MLA_KV_RANK, cols), F32),
                        pltpu.VMEM((2, 3, tk, cols), F32), pltpu.VMEM((2, 3, tk, cols), BF16),
                        pltpu.VMEM((2, 3, SUBLANES, cols), F32)],
        compiler_params=pltpu.CompilerParams(
            dimension_semantics=("parallel", "arbitrary"), vmem_limit_bytes=VMEM_LIMIT),
        name="mla",
    )(qm, qm, km, kvt, kmeta, kvtmeta, wuv, smz, smz)


def _swap_halves(w):
    half = w.shape[-1] // 2
    return jnp.concatenate([w[..., half:], w[..., :half]], axis=-1)


def _prep_weights(w_in, gla_gate_w, mla_w_uq, mla_w_ukv):
    cuts = [0]
    for s in SPLITS:
        cuts.append(cuts[-1] + s)
    (g_q, g_k, g_v, g_lr, g_z, m_cq, m_ckv, m_kr, m_z, gate_gla, gate_mla) = [
        w_in[:, cuts[n]:cuts[n + 1]] for n in range(len(SPLITS))]
    m_krs = _swap_halves(m_kr)
    pad = jnp.zeros((D_MODEL, LANES - GLA_GATE_RANK), w_in.dtype)
    small = jnp.concatenate([m_cq, m_ckv, m_kr, m_kr, m_krs, m_krs, g_lr, pad], axis=1)
    assert small.shape == (D_MODEL, W_SMALL)
    wide = tuple(w.astype(BF16) for w in (g_q, g_k, g_v, g_z, m_z, gate_gla, gate_mla, small))

    uq = mla_w_uq.reshape(MLA_Q_RANK, MLA_HEADS, MLA_QK)
    uq_nope = uq[:, :, :MLA_NOPE].reshape(MLA_Q_RANK, MLA_HEADS * MLA_NOPE)
    uq_rope = uq[:, :, MLA_NOPE:]
    wuq = jnp.concatenate([uq_nope, uq_rope.reshape(MLA_Q_RANK, -1),
                           _swap_halves(uq_rope).reshape(MLA_Q_RANK, -1)], axis=1).astype(BF16)

    ukv = mla_w_ukv.reshape(MLA_KV_RANK, MLA_HEADS, MLA_NOPE + MLA_DV)
    wukt = jnp.transpose(ukv[:, :, :MLA_NOPE], (1, 2, 0)).astype(BF16)
    zero = jnp.zeros_like(wukt[0::2])
    wukt = jnp.concatenate([jnp.concatenate([wukt[0::2], zero], axis=2),
                            jnp.concatenate([zero, wukt[1::2]], axis=2)], axis=1)
    wuv = jnp.transpose(ukv[:, :, MLA_NOPE:], (1, 0, 2)).astype(BF16)

    gw = jnp.concatenate([gla_gate_w, jnp.zeros((LANES - GLA_GATE_RANK, GLA_KW), gla_gate_w.dtype)],
                         axis=0).astype(BF16)
    return wide, wuq, wukt, wuv, gw


def _rope_tables(n):
    inv = 1.0 / (ROPE_BASE ** (jnp.arange(0, MLA_ROPE, 2, dtype=F32) / MLA_ROPE))
    ang = jnp.arange(n, dtype=F32)[:, None] * inv[None, :]
    cos, sin = jnp.cos(ang), jnp.sin(ang)
    cos = jnp.concatenate([cos, cos, cos, cos], axis=1)
    sin = jnp.concatenate([-sin, sin, -sin, sin], axis=1)
    return cos, sin


def kernel(x, meta_tokens, norm_g, w_in, gla_gate_w, gla_gate_b, gla_norm_g, gla_proj,
           mla_q_norm_g, mla_w_uq, mla_kv_norm_g, mla_w_ukv, mla_proj, w_out, final_norm_g):
    B, S, D = x.shape
    assert D == D_MODEL and norm_g.shape[0] == 1 and meta_tokens.shape == (N_META, D_MODEL)

    wide, wuq, wukt, wuv, gw = _prep_weights(w_in[0], gla_gate_w[0], mla_w_uq[0], mla_w_ukv[0])
    row = lambda a: a.reshape(1, -1).astype(F32)
    ng, qg, kvg = row(norm_g[0]), row(mla_q_norm_g[0]), row(mla_kv_norm_g[0])
    gb, gng, fng = row(gla_gate_b[0]), row(gla_norm_g[0]), row(final_norm_g)
    cos, sin = _rope_tables(N_META + S)

    proj_args = (ng, wide, qg, kvg, wuq, wukt)
    meta = _inproj(meta_tokens[None].astype(F32), *proj_args, cos[:N_META], sin[:N_META],
                   tm=N_META, emit_vt=False)
    _, k_m, v_m, _, _, _, _, glr_m, _, kmeta = meta
    kvtmeta = jnp.concatenate([jnp.swapaxes(kmeta[:, :, 0:MLA_KV_RANK], 1, 2),
                               jnp.ones((1, MLA_VT_ROWS - MLA_KV_RANK, N_META), BF16)], axis=1)
    q, k, v, sz, smz, sgg, sgm, glr, qm, km, kvt = _inproj(
        x, *proj_args, cos[N_META:], sin[N_META:], tm=512, emit_vt=True)

    yba, ybb = _mla(qm, km, kvt, kmeta, kvtmeta, wuv, smz, tq=256, tk=256)
    return _tail(q, k, v, glr, sz, k_m, v_m, glr_m, gw, gb, gng, yba, ybb, sgg, sgm, x,
                 gla_proj[0].astype(BF16), mla_proj[0].astype(BF16), w_out[0].astype(BF16), fng, tc=512)
```

```python
import functools
import math

import jax
import jax.numpy as jnp
from jax import lax
from jax.experimental import pallas as pl
from jax.experimental.pallas import tpu as pltpu

F32 = jnp.float32
BF16 = jnp.bfloat16

D_MODEL = 1024
N_META = 16
EPS = 1e-6

GLA_HEADS = 4
GLA_DK = 128
GLA_DV = 256
GLA_GATE_RANK = 16
GLA_GATE_NORMALIZER = 16.0
GLA_CHUNK = 64
GLA_KW = GLA_HEADS * GLA_DK
GLA_VW = GLA_HEADS * GLA_DV

MLA_HEADS = 8
MLA_NOPE = 128
MLA_ROPE = 64
MLA_DV = 128
MLA_Q_RANK = 256
MLA_KV_RANK = 128
MLA_QK = MLA_NOPE + MLA_ROPE
MLA_VW = MLA_HEADS * MLA_DV
ROPE_BASE = 10000.0

SPLITS = (GLA_KW, GLA_KW, GLA_VW, GLA_GATE_RANK, GLA_VW,
          MLA_Q_RANK, MLA_KV_RANK, MLA_ROPE, MLA_VW, D_MODEL, D_MODEL)

LANES = 128
SUBLANES = 8
SOFTMAX_SLAB = 32
MXU_TILE = 256
MLA_KPAD = 2 * LANES
MLA_VT_ROWS = MLA_KV_RANK + 16
W_SMALL = MLA_Q_RANK + MLA_KV_RANK + 3 * LANES
VMEM_LIMIT = 56 * 1024 * 1024


def _rms(x, g):
    return x * lax.rsqrt(jnp.mean(x * x, axis=-1, keepdims=True) + EPS) * g


def _dot(a, b):
    return jnp.dot(a, b, preferred_element_type=F32)


def _dot_nt(a, b):
    return lax.dot_general(a, b, (((1,), (1,)), ((), ())), preferred_element_type=F32)


def _dot_tn(a, b):
    return lax.dot_general(a, b, (((0,), (0,)), ((), ())), preferred_element_type=F32)


def _inproj_kernel(x_ref, ng_ref, wq_ref, wk_ref, wv_ref, wgz_ref, wmz_ref, wgg_ref, wgm_ref, wsmall_ref,
                   qg_ref, kvg_ref, wuq_ref, wukt_ref, cos_ref, sin_ref,
                   q_ref, k_ref, v_ref, sz_ref, smz_ref, sgg_ref, sgm_ref, glr_ref, qm_ref, km_ref,
                   *maybe_kvt_ref):
    x = x_ref[0]
    u = (x * ng_ref[...]).astype(BF16)
    inv_rms = lax.rsqrt(jnp.mean(x * x, axis=-1, keepdims=True) + EPS)

    half_inv_rms = 0.5 * inv_rms

    def proj(w_ref):
        return _dot(u, w_ref[...]) * inv_rms

    def silu_proj(w_ref):
        h = _dot(u, w_ref[...]) * half_inv_rms
        return h * jnp.tanh(h) + h

    def sigmoid_proj(w_ref):
        return 0.5 * jnp.tanh(_dot(u, w_ref[...]) * half_inv_rms) + 0.5

    small = proj(wsmall_ref)
    cq = small[:, 0:MLA_Q_RANK]
    ckv = small[:, MLA_Q_RANK:MLA_Q_RANK + MLA_KV_RANK]
    o = MLA_Q_RANK + MLA_KV_RANK
    kr = small[:, o:o + LANES]
    krs = small[:, o + LANES:o + 2 * LANES]
    glr_ref[0] = small[:, o + 2 * LANES:o + 3 * LANES].astype(BF16)

    cos = cos_ref[...]
    sin = sin_ref[...]
    ckvn = _rms(ckv, kvg_ref[...])
    km_ref[0, :, 0:LANES] = ckvn.astype(BF16)
    for kvt_ref in maybe_kvt_ref:
        kvt_ref[0, 0:MLA_KV_RANK, :] = ckvn.T.astype(BF16)
        kvt_ref[0, MLA_KV_RANK:MLA_VT_ROWS, :] = jnp.ones((MLA_VT_ROWS - MLA_KV_RANK, x.shape[0]), BF16)
    km_ref[0, :, LANES:MLA_KPAD] = (kr * cos + krs * sin).astype(BF16)
    cqn = _rms(cq, qg_ref[...]).astype(BF16)

    sz_ref[0] = silu_proj(wgz_ref).astype(BF16)
    qall = _dot(cqn, wuq_ref[...])
    smz_ref[0] = silu_proj(wmz_ref).astype(BF16)

    scale = MLA_QK ** -0.5 * math.log2(math.e)
    nope_w = MLA_HEADS * MLA_NOPE
    rope_w = MLA_HEADS * MLA_ROPE
    lane = lax.broadcasted_iota(jnp.int32, (x.shape[0], LANES), 1)
    for g in range(MLA_HEADS // 2):
        qn = qall[:, 2 * g * MLA_NOPE:(2 * g + 2) * MLA_NOPE].astype(BF16)
        qa = _dot(qn, wukt_ref[g]) * scale
        qm_ref[0, 2 * g, :, 0:LANES] = qa[:, 0:LANES].astype(BF16)
        qm_ref[0, 2 * g + 1, :, 0:LANES] = qa[:, LANES:2 * LANES].astype(BF16)
        a = qall[:, nope_w + g * LANES:nope_w + (g + 1) * LANES]
        b = qall[:, nope_w + rope_w + g * LANES:nope_w + rope_w + (g + 1) * LANES]
        r = (a * cos + b * sin) * scale
        qm_ref[0, 2 * g, :, LANES:MLA_KPAD] = jnp.where(lane < MLA_ROPE, r, 0.0).astype(BF16)
        qm_ref[0, 2 * g + 1, :, LANES:MLA_KPAD] = jnp.where(lane >= MLA_ROPE, r, 0.0).astype(BF16)

    sgg_ref[0] = sigmoid_proj(wgg_ref).astype(BF16)
    sgm_ref[0] = sigmoid_proj(wgm_ref).astype(BF16)
    q_ref[0] = (proj(wq_ref) * (GLA_DK ** -0.5)).astype(BF16)
    v_ref[0] = proj(wv_ref).astype(BF16)
    k_ref[0] = proj(wk_ref).astype(BF16)


def _inproj(x, ng, wide, qg, kvg, wuq, wukt, cos, sin, tm, emit_vt):
    B, S, _ = x.shape
    grid = (B, S // tm)
    tok = lambda w: pl.BlockSpec((1, tm, w), lambda b, i: (b, i, 0))
    full = lambda a: pl.BlockSpec(a.shape, lambda b, i: (0,) * a.ndim)
    out_shapes = [
        jax.ShapeDtypeStruct((B, S, GLA_KW), BF16),
        jax.ShapeDtypeStruct((B, S, GLA_KW), BF16),
        jax.ShapeDtypeStruct((B, S, GLA_VW), BF16),
        jax.ShapeDtypeStruct((B, S, GLA_VW), BF16),
        jax.ShapeDtypeStruct((B, S, MLA_VW), BF16),
        jax.ShapeDtypeStruct((B, S, D_MODEL), BF16),
        jax.ShapeDtypeStruct((B, S, D_MODEL), BF16),
        jax.ShapeDtypeStruct((B, S, LANES), BF16),
        jax.ShapeDtypeStruct((B, MLA_HEADS, S, MLA_KPAD), BF16),
        jax.ShapeDtypeStruct((B, S, MLA_KPAD), BF16),
    ]
    out_specs = [tok(GLA_KW), tok(GLA_KW), tok(GLA_VW), tok(GLA_VW), tok(MLA_VW),
                 tok(D_MODEL), tok(D_MODEL), tok(LANES),
                 pl.BlockSpec((1, MLA_HEADS, tm, MLA_KPAD), lambda b, i: (b, 0, i, 0)),
                 tok(MLA_KPAD)]
    if emit_vt:
        out_shapes.append(jax.ShapeDtypeStruct((B, MLA_VT_ROWS, S), BF16))
        out_specs.append(pl.BlockSpec((1, MLA_VT_ROWS, tm), lambda b, i: (b, 0, i)))
    tab = pl.BlockSpec((tm, LANES), lambda b, i: (i, 0))
    return pl.pallas_call(
        _inproj_kernel,
        grid=grid,
        in_specs=[tok(D_MODEL), full(ng), *[full(w) for w in wide],
                  full(qg), full(kvg), full(wuq), full(wukt), tab, tab],
        out_specs=out_specs,
        out_shape=out_shapes,
        compiler_params=pltpu.CompilerParams(
            dimension_semantics=("parallel", "parallel"), vmem_limit_bytes=VMEM_LIMIT),
        name="inproj",
    )(x, ng, *wide, qg, kvg, wuq, wukt, cos, sin)


def _tri(n):
    row = lax.broadcasted_iota(jnp.int32, (n, n), 0)
    col = lax.broadcasted_iota(jnp.int32, (n, n), 1)
    return col <= row


def _log_sigmoid(z):
    return jnp.minimum(z, 0.0) - jnp.log1p(jnp.exp(-jnp.abs(z)))


def _gla_log_gates(glr, gw_ref, gb_ref):
    return _log_sigmoid(_dot(glr, gw_ref[...]) + gb_ref[...]) / GLA_GATE_NORMALIZER


def _cumsum_rows(tri, g):
    hi = g.astype(BF16)
    lo = (g - hi.astype(F32)).astype(BF16)
    return _dot(tri, hi) + _dot(tri, lo)


def _tail_kernel(q_ref, k_ref, v_ref, glr_ref, sz_ref, km_ref, vm_ref, glrm_ref, gw_ref, gb_ref, gng_ref,
                 yba_ref, ybb_ref, sgg_ref, sgm_ref, x_ref, gp_ref, mp_ref, wo_ref, fng_ref,
                 o_ref, st_ref, oi_ref, u_ref, ya_ref, yacc_ref, merged_ref, *, n_chunks, tiles_per_row):
    C = GLA_CHUNK
    step = pl.program_id(0)
    n_tiles = pl.num_programs(0) - 1
    tile_in_row = jnp.minimum(step, n_tiles - 1) % tiles_per_row
    prev_in_first_half = jnp.maximum(step - 1, 0) % tiles_per_row < tiles_per_row // 2

    @pl.when(step == 0)
    def _no_previous_tile():
        ya_ref[...] = jnp.zeros(ya_ref.shape, BF16)

    @pl.when(tile_in_row == 0)
    def _init_state():
        b = _cumsum_rows(_tri(N_META).astype(BF16), _gla_log_gates(glrm_ref[0], gw_ref, gb_ref))
        kl = (km_ref[0].astype(F32) * jnp.exp(b[-1:, :] - b)).astype(BF16)
        for h in range(GLA_HEADS):
            st_ref[h] = _dot_tn(vm_ref[0, :, h * GLA_DV:(h + 1) * GLA_DV],
                                kl[:, h * GLA_DK:(h + 1) * GLA_DK])

    causal = _tri(C)
    tri = causal.astype(BF16)
    gng = gng_ref[...]
    heads = [(slice(h * GLA_DK, (h + 1) * GLA_DK), slice(h * GLA_DV, (h + 1) * GLA_DV))
             for h in range(GLA_HEADS)]
    chunks = [slice(c * C, (c + 1) * C) for c in range(n_chunks)]
    col_blocks = [slice(c, c + MXU_TILE) for c in range(0, D_MODEL, MXU_TILE)]

    def merge_block(cs, rs):
        yb = jnp.where(prev_in_first_half, yba_ref[0, rs, :], ybb_ref[0, rs, :])
        y_b = _dot(yb, mp_ref[:, cs])
        merged = (sgg_ref[0, rs, cs].astype(F32) * yacc_ref[rs, cs]
                  + sgm_ref[0, rs, cs].astype(F32) * y_b)
        merged_ref[rs, cs] = merged.astype(BF16)

    def out_block(cs, rs):
        h = x_ref[0, rs, cs] + _dot(merged_ref[rs, :], wo_ref[:, cs])
        o_ref[0, rs, cs] = h
        return jnp.sum(h * h, axis=-1, keepdims=True)

    gate_logits = _dot(glr_ref[0], gw_ref[...]) + gb_ref[...]
    for cs in col_blocks:
        yacc_ref[:, cs] = _dot(ya_ref[...], gp_ref[:, cs])

    g = _log_sigmoid(gate_logits) / GLA_GATE_NORMALIZER
    b = jnp.concatenate([_cumsum_rows(tri, g[rows]) for rows in chunks], axis=0)
    b3 = b.reshape(n_chunks, C, GLA_KW)
    b_last = b3[:, C - 1:C, :]
    q = q_ref[0].astype(F32)
    k = k_ref[0].astype(F32)
    qe = (q * jnp.exp(b)).astype(BF16)
    ke = (k * jnp.exp(-b)).astype(BF16)
    kl = (k * jnp.exp(b_last - b3).reshape(n_chunks * C, GLA_KW)).astype(BF16)
    decay = jnp.exp(b_last)

    for cs in col_blocks[:-1]:
        merge_block(cs, slice(None))

    blocks = [(c, rows, h, ks, vs) for c, rows in enumerate(chunks) for h, (ks, vs) in enumerate(heads)]
    a = [jnp.where(causal, _dot_nt(qe[rows, ks], ke[rows, ks]), 0.0).astype(BF16)
         for _, rows, _, ks, _ in blocks]
    for c, rows, h, ks, vs in blocks:
        u_ref[c, h] = _dot_tn(v_ref[0, rows, vs], kl[rows, ks])
    for a_ch, (_, rows, _, _, vs) in zip(a, blocks):
        oi_ref[rows, vs] = _dot(a_ch, v_ref[0, rows, vs])

    def gla_chunk(c):
        rows = chunks[c]
        for h, (ks, vs) in enumerate(heads):
            st = st_ref[h]
            o = oi_ref[rows, vs] + _dot_nt(qe[rows, ks], st.astype(BF16))
            st_ref[h] = st * decay[c, :, ks] + u_ref[c, h]
            o = _rms(o, gng) * sz_ref[0, rows, vs].astype(F32)
            ya_ref[rows, vs] = o.astype(BF16)

    half = n_chunks * C // 2
    halves = [slice(0, half), slice(half, 2 * half)]
    def final_norm(rs, sumsq):
        o_ref[0, rs, :] = o_ref[0, rs, :] * lax.rsqrt(sumsq / D_MODEL + EPS) * fng_ref[...]

    work = [functools.partial(merge_block, col_blocks[-1], rs) for rs in halves]
    work += [functools.partial(out_block, cs, rs) for rs in halves for cs in col_blocks]
    n_lead = len(work) - n_chunks
    assert n_lead >= len(halves)
    for w in work[:len(halves)]:
        w()
    sums = []
    for n, w in enumerate(work[len(halves):]):
        sums.append(w())
        if len(sums) == len(col_blocks):
            final_norm(halves[n // len(col_blocks)], functools.reduce(jnp.add, sums))
            sums = []
        if n + len(halves) >= n_lead:
            gla_chunk(n + len(halves) - n_lead)


def _tail(q, k, v, glr, sz, km, vm, glrm, gw, gb, gng, yba, ybb, sgg, sgm, x, gp, mp, wo, fng, tc):
    B, S, _ = q.shape
    n = S // tc
    n_tiles = B * n
    gla_tile = lambda s: jnp.minimum(s, n_tiles - 1)
    out_tile = lambda s: jnp.maximum(s - 1, 0)
    gtok = lambda w: pl.BlockSpec((1, tc, w), lambda s: (gla_tile(s) // n, gla_tile(s) % n, 0))
    otok = pl.BlockSpec((1, tc, D_MODEL), lambda s: (out_tile(s) // n, out_tile(s) % n, 0))
    ytok_a = pl.BlockSpec((1, tc, D_MODEL),
                          lambda s: (out_tile(s) // n, jnp.minimum(out_tile(s) % n, n // 2 - 1), 0))
    ytok_b = pl.BlockSpec((1, tc, D_MODEL),
                          lambda s: (out_tile(s) // n, jnp.maximum(out_tile(s) % n - n // 2, 0), 0))
    full = lambda a: pl.BlockSpec(a.shape, lambda s: (0,) * a.ndim)
    n_chunks = tc // GLA_CHUNK
    return pl.pallas_call(
        functools.partial(_tail_kernel, n_chunks=n_chunks, tiles_per_row=n),
        grid=(n_tiles + 1,),
        in_specs=[gtok(GLA_KW), gtok(GLA_KW), gtok(GLA_VW), gtok(LANES), gtok(GLA_VW),
                  full(km), full(vm), full(glrm), full(gw), full(gb), full(gng),
                  ytok_a, ytok_b, otok, otok, otok, full(gp), full(mp), full(wo), full(fng)],
        out_specs=otok,
        out_shape=jax.ShapeDtypeStruct((B, S, D_MODEL), F32),
        scratch_shapes=[pltpu.VMEM((GLA_HEADS, GLA_DV, GLA_DK), F32),
                        pltpu.VMEM((tc, GLA_VW), F32),
                        pltpu.VMEM((n_chunks, GLA_HEADS, GLA_DV, GLA_DK), F32),
                        pltpu.VMEM((tc, GLA_VW), BF16),
                        pltpu.VMEM((tc, D_MODEL), F32),
                        pltpu.VMEM((tc, D_MODEL), BF16)],
        compiler_params=pltpu.CompilerParams(
            dimension_semantics=("arbitrary",), vmem_limit_bytes=VMEM_LIMIT),
        name="tail",
    )(q, k, v, glr, sz, km, vm, glrm, gw, gb, gng, yba, ybb, sgg, sgm, x, gp, mp, wo, fng)


def _mla_kernel(qa_ref, qb_ref, km_ref, kvt_ref, kmeta_ref, kvtmeta_ref, wuv_ref, smza_ref, smzb_ref,
                oa_ref, ob_ref, m_ref, l_ref, alpha_ref, acc_ref, s_ref, p_ref, bmax_ref, *, tq, tk, n_tiles):
    assert tq == tk
    cols = MLA_HEADS * tq
    gw = MLA_KPAD
    hpg = gw // tq
    groups = [(g, slice(g * gw, (g + 1) * gw)) for g in range(cols // gw)]
    META_BUF = 2

    def keys(k0):
        return km_ref[0, k0:k0 + tk, :]

    def values(k0):
        return kvt_ref[0, :, k0:k0 + tk]

    def fold(x, op):
        return functools.reduce(op, [x[r:r + SUBLANES] for r in range(0, x.shape[0], SUBLANES)])

    def tile_stages(tile, slot, q_ref, smz_ref, o_ref):
        nf = tile
        diag0 = nf * tk
        block_start = lambda t: diag0 if t == 0 else (t - 1) * tk

        def scores(kb, g):
            return _dot_nt(kb, q_ref[0, g * hpg:(g + 1) * hpg].reshape(gw, MLA_KPAD))

        def add_values(vt, p, cs):
            r = _dot(vt, p)
            alpha = alpha_ref[slot, :, cs]
            acc_ref[slot, :, cs] = alpha * acc_ref[slot, :, cs] + r[0:MLA_KV_RANK]
            l_ref[slot, :, cs] = alpha * l_ref[slot, :, cs] + r[MLA_KV_RANK:MLA_KV_RANK + 1]

        def put_scores(buf, n, cs, s):
            s_ref[slot, buf, 0:n, cs] = s
            bmax_ref[slot, buf, :, cs] = fold(s, jnp.maximum)

        def softmax(buf, n, cs):
            slab = min(n, SOFTMAX_SLAB)
            m_old = m_ref[slot, :, cs]
            m_new = jnp.maximum(m_old, jnp.max(bmax_ref[slot, buf, :, cs], axis=0, keepdims=True))
            alpha = jnp.exp2(m_old - m_new)
            for r in range(0, n, slab):
                p_ref[slot, buf, r:r + slab, cs] = jnp.exp2(
                    s_ref[slot, buf, r:r + slab, cs] - m_new).astype(BF16)
            m_ref[slot, :, cs] = m_new
            alpha_ref[slot, :, cs] = alpha

        def start():
            m_ref[slot] = jnp.full(m_ref.shape[1:], -jnp.inf, F32)
            l_ref[slot] = jnp.zeros(l_ref.shape[1:], F32)
            alpha_ref[slot] = jnp.zeros(alpha_ref.shape[1:], F32)
            acc_ref[slot] = jnp.zeros(acc_ref.shape[1:], F32)

        def diagonal(g, cs):
            ktok = diag0 + lax.broadcasted_iota(jnp.int32, (tk, gw), 0)
            qtok = tile * tq + (lax.broadcasted_iota(jnp.int32, (tk, gw), 1) & (tq - 1))
            s = scores(jnp.concatenate([keys(diag0), kmeta_ref[0]], axis=0), g)
            put_scores(0, tk, cs, jnp.where(ktok <= qtok, s[0:tk], -jnp.inf))
            put_scores(META_BUF, N_META, cs, s[tk:tk + N_META])

        def block(t, g, cs):
            cur, nxt = t % 2, 1 - t % 2
            if t > 0:
                add_values(values(block_start(t - 1)), p_ref[slot, nxt, :, cs], cs)
            if t < nf:
                put_scores(nxt, tk, cs, scores(keys(block_start(t + 1)), g))
            softmax(cur, tk, cs)

        def finish(g, cs):
            add_values(values(block_start(nf)), p_ref[slot, nf % 2, :, cs], cs)
            softmax(META_BUF, N_META, cs)
            add_values(kvtmeta_ref[0], p_ref[slot, META_BUF, 0:N_META, cs], cs)

        def head_out(h):
            hs = slice(h * tq, (h + 1) * tq)
            vs = slice(h * MLA_DV, (h + 1) * MLA_DV)
            o_lat = (acc_ref[slot, :, hs] / l_ref[slot, :, hs]).astype(BF16)
            o = _dot_tn(o_lat, wuv_ref[h])
            o_ref[0, :, vs] = (o * smz_ref[0, :, vs].astype(F32)).astype(BF16)

        stages = [start]
        stages += [functools.partial(diagonal, g, cs) for g, cs in groups]
        stages += [functools.partial(block, t, g, cs) for t in range(nf + 1) for g, cs in groups]
        stages += [functools.partial(finish, g, cs) for g, cs in groups]
        stages += [functools.partial(head_out, h) for h in range(MLA_HEADS)]
        return stages

    def run_pair(j):
        a = tile_stages(j, 0, qa_ref, smza_ref, oa_ref)
        b = tile_stages(n_tiles - 1 - j, 1, qb_ref, smzb_ref, ob_ref)
        ia = ib = 0
        while ia < len(a) or ib < len(b):
            if ib >= len(b) or (ia < len(a) and (ia + 1) * len(b) <= 2 * (ib + 1) * len(a)):
                a[ia](); ia += 1
            else:
                b[ib](); ib += 1

    for j in range(n_tiles // 2):
        pl.when(pl.program_id(1) == j)(functools.partial(run_pair, j))


def _mla(qm, km, kvt, kmeta, kvtmeta, wuv, smz, tq, tk):
    B, H, S, _ = qm.shape
    nq = S // tq
    assert nq % 2 == 0 and S % tk == 0
    grid = (B, nq // 2)
    full = lambda a: pl.BlockSpec(a.shape, lambda b, j: (0,) * a.ndim)
    cols = H * tq
    half = jax.ShapeDtypeStruct((B, S // 2, MLA_VW), BF16)
    return pl.pallas_call(
        functools.partial(_mla_kernel, tq=tq, tk=tk, n_tiles=nq),
        grid=grid,
        in_specs=[pl.BlockSpec((1, H, tq, MLA_KPAD), lambda b, j: (b, 0, j, 0)),
                  pl.BlockSpec((1, H, tq, MLA_KPAD), lambda b, j: (b, 0, nq - 1 - j, 0)),
                  pl.BlockSpec((1, S, MLA_KPAD), lambda b, j: (b, 0, 0)),
                  pl.BlockSpec((1, MLA_VT_ROWS, S), lambda b, j: (b, 0, 0)),
                  full(kmeta), full(kvtmeta), full(wuv),
                  pl.BlockSpec((1, tq, MLA_VW), lambda b, j: (b, j, 0)),
                  pl.BlockSpec((1, tq, MLA_VW), lambda b, j: (b, nq - 1 - j, 0))],
        out_specs=[pl.BlockSpec((1, tq, MLA_VW), lambda b, j: (b, j, 0)),
                   pl.BlockSpec((1, tq, MLA_VW), lambda b, j: (b, nq // 2 - 1 - j, 0))],
        out_shape=[half, half],
        scratch_shapes=[pltpu.VMEM((2, 1, cols), F32), pltpu.VMEM((2, 1, cols), F32),
                        pltpu.VMEM((2, 1, cols), F32),
                        pltpu.VMEM((2, MLA_KV_RANK, cols), F32),
                        pltpu.VMEM((2, 3, tk, cols), F32), pltpu.VMEM((2, 3, tk, cols), BF16),
                        pltpu.VMEM((2, 3, SUBLANES, cols), F32)],
        compiler_params=pltpu.CompilerParams(
            dimension_semantics=("parallel", "arbitrary"), vmem_limit_bytes=VMEM_LIMIT),
        name="mla",
    )(qm, qm, km, kvt, kmeta, kvtmeta, wuv, smz, smz)


def _swap_halves(w):
    half = w.shape[-1] // 2
    return jnp.concatenate([w[..., half:], w[..., :half]], axis=-1)


def _prep_weights(w_in, gla_gate_w, mla_w_uq, mla_w_ukv):
    cuts = [0]
    for s in SPLITS:
        cuts.append(cuts[-1] + s)
    (g_q, g_k, g_v, g_lr, g_z, m_cq, m_ckv, m_kr, m_z, gate_gla, gate_mla) = [
        w_in[:, cuts[n]:cuts[n + 1]] for n in range(len(SPLITS))]
    m_krs = _swap_halves(m_kr)
    pad = jnp.zeros((D_MODEL, LANES - GLA_GATE_RANK), w_in.dtype)
    small = jnp.concatenate([m_cq, m_ckv, m_kr, m_kr, m_krs, m_krs, g_lr, pad], axis=1)
    assert small.shape == (D_MODEL, W_SMALL)
    wide = tuple(w.astype(BF16) for w in (g_q, g_k, g_v, g_z, m_z, gate_gla, gate_mla, small))

    uq = mla_w_uq.reshape(MLA_Q_RANK, MLA_HEADS, MLA_QK)
    uq_nope = uq[:, :, :MLA_NOPE].reshape(MLA_Q_RANK, MLA_HEADS * MLA_NOPE)
    uq_rope = uq[:, :, MLA_NOPE:]
    wuq = jnp.concatenate([uq_nope, uq_rope.reshape(MLA_Q_RANK, -1),
                           _swap_halves(uq_rope).reshape(MLA_Q_RANK, -1)], axis=1).astype(BF16)

    ukv = mla_w_ukv.reshape(MLA_KV_RANK, MLA_HEADS, MLA_NOPE + MLA_DV)
    wukt = jnp.transpose(ukv[:, :, :MLA_NOPE], (1, 2, 0)).astype(BF16)
    zero = jnp.zeros_like(wukt[0::2])
    wukt = jnp.concatenate([jnp.concatenate([wukt[0::2], zero], axis=2),
                            jnp.concatenate([zero, wukt[1::2]], axis=2)], axis=1)
    wuv = jnp.transpose(ukv[:, :, MLA_NOPE:], (1, 0, 2)).astype(BF16)

    gw = jnp.concatenate([gla_gate_w, jnp.zeros((LANES - GLA_GATE_RANK, GLA_KW), gla_gate_w.dtype)],
                         axis=0).astype(BF16)
    return wide, wuq, wukt, wuv, gw


def _rope_tables(n):
    inv = 1.0 / (ROPE_BASE ** (jnp.arange(0, MLA_ROPE, 2, dtype=F32) / MLA_ROPE))
    ang = jnp.arange(n, dtype=F32)[:, None] * inv[None, :]
    cos, sin = jnp.cos(ang), jnp.sin(ang)
    cos = jnp.concatenate([cos, cos, cos, cos], axis=1)
    sin = jnp.concatenate([-sin, sin, -sin, sin], axis=1)
    return cos, sin


def kernel(x, meta_tokens, norm_g, w_in, gla_gate_w, gla_gate_b, gla_norm_g, gla_proj,
           mla_q_norm_g, mla_w_uq, mla_kv_norm_g, mla_w_ukv, mla_proj, w_out, final_norm_g):
    B, S, D = x.shape
    assert D == D_MODEL and norm_g.shape[0] == 1 and meta_tokens.shape == (N_META, D_MODEL)

    wide, wuq, wukt, wuv, gw = _prep_weights(w_in[0], gla_gate_w[0], mla_w_uq[0], mla_w_ukv[0])
    row = lambda a: a.reshape(1, -1).astype(F32)
    ng, qg, kvg = row(norm_g[0]), row(mla_q_norm_g[0]), row(mla_kv_norm_g[0])
    gb, gng, fng = row(gla_gate_b[0]), row(gla_norm_g[0]), row(final_norm_g)
    cos, sin = _rope_tables(N_META + S)

    proj_args = (ng, wide, qg, kvg, wuq, wukt)
    meta = _inproj(meta_tokens[None].astype(F32), *proj_args, cos[:N_META], sin[:N_META],
                   tm=N_META, emit_vt=False)
    _, k_m, v_m, _, _, _, _, glr_m, _, kmeta = meta
    kvtmeta = jnp.concatenate([jnp.swapaxes(kmeta[:, :, 0:MLA_KV_RANK], 1, 2),
                               jnp.ones((1, MLA_VT_ROWS - MLA_KV_RANK, N_META), BF16)], axis=1)
    q, k, v, sz, smz, sgg, sgm, glr, qm, km, kvt = _inproj(
        x, *proj_args, cos[N_META:], sin[N_META:], tm=512, emit_vt=True)

    yba, ybb = _mla(qm, km, kvt, kmeta, kvtmeta, wuv, smz, tq=256, tk=256)
    return _tail(q, k, v, glr, sz, k_m, v_m, glr_m, gw, gb, gng, yba, ybb, sgg, sgm, x,
                 gla_proj[0].astype(BF16), mla_proj[0].astype(BF16), w_out[0].astype(BF16), fng, tc=512)
```
